```python
import math
import jax, jax.numpy as jnp
from jax import lax
import numpy as np

D_MODEL = 4096
BATCH = 1
SEQ = 8192
DEPTH = 2
DEC_BATCH = 8
DEC_SEQ = 16
PAST_LEN = 1024

CHUNK = 64
N_MIXERS = 2
N_POOL_LAYERS = (DEPTH + 1) // 2
N_SSM_LAYERS = DEPTH // 2
POOL_WINDOWS = (2, 4, 8, 16)
N_POOL_GROUPS = len(POOL_WINDOWS)
POOL_GROUP = D_MODEL // N_POOL_GROUPS
POOL_HIST = max(POOL_WINDOWS) - 1
SSM_GROUP = 16
N_SSM_GROUPS = D_MODEL // SSM_GROUP
SSM_STATE = 64
D_FF = 4 * D_MODEL
EPS = 1e-6
DT_MIN = 1e-3
DT_MAX = 1e-1

kernel_name = "hybrid_pool_s5_streaming_step"


def rmsnorm(x, g):
    xf = x.astype(jnp.float32)
    y = xf * lax.rsqrt(jnp.mean(xf * xf, axis=-1, keepdims=True) + EPS) * g.astype(jnp.float32)
    return y.astype(x.dtype)


def pool_mixer(u, hist, pos0, w_pool, scale):
    b, L, _ = u.shape
    ext = jnp.concatenate([hist.astype(u.dtype), u], axis=1)
    cs = jnp.cumsum(jnp.pad(ext.astype(jnp.float32), ((0, 0), (1, 0), (0, 0))), axis=1)
    end = cs[:, POOL_HIST + 1:POOL_HIST + 1 + L]
    pos = pos0 + jnp.arange(L, dtype=jnp.int32)
    means = []
    for gi, w in enumerate(POOL_WINDOWS):
        sl = slice(gi * POOL_GROUP, (gi + 1) * POOL_GROUP)
        start = cs[:, POOL_HIST + 1 - w:POOL_HIST + 1 - w + L, sl]
        cnt = jnp.minimum(w, pos + 1).astype(jnp.float32)[None, :, None]
        means.append((end[..., sl] - start) / cnt)
    pooled = jnp.concatenate(means, axis=-1) - u.astype(jnp.float32)
    mixed = jnp.einsum('blgc,gcd->blgd', pooled.reshape(b, L, N_POOL_GROUPS, POOL_GROUP),
                       w_pool.astype(jnp.float32)).reshape(b, L, D_MODEL)
    out = (mixed * scale.astype(jnp.float32)).astype(u.dtype)
    new_hist = ext[:, -POOL_HIST:]
    return out, new_hist


def _ssm_combine(left, right):
    a_l, b_l = left
    a_r, b_r = right
    return a_r * a_l, a_r * b_l + b_r


def ssm_mixer(u, h0_re, h0_im, a_re, a_im, log_dt, b_re, b_im, c_re, c_im, d_skip, w_glu_a, w_glu_b):
    b, L, _ = u.shape
    f32 = jnp.float32
    lam = lax.complex(a_re.astype(f32), a_im.astype(f32))
    dt = jnp.exp(log_dt.astype(f32))[:, None]
    a_bar = jnp.exp(lam * dt)
    b_bar = ((a_bar - 1.0) / lam)[..., None] * lax.complex(b_re.astype(f32), b_im.astype(f32))
    uf = u.astype(f32)
    ug = uf.reshape(b, L, N_SSM_GROUPS, SSM_GROUP).astype(jnp.complex64)
    bu = jnp.einsum('gpc,blgc->blgp', b_bar, ug)
    a_seq = jnp.broadcast_to(a_bar, bu.shape)
    a_cum, h = lax.associative_scan(_ssm_combine, (a_seq, bu), axis=1)
    h0 = lax.complex(h0_re.astype(f32), h0_im.astype(f32))
    h = h + a_cum * h0[:, None]
    c = lax.complex(c_re.astype(f32), c_im.astype(f32))
    y = jnp.real(jnp.einsum('gcp,blgp->blgc', c, h)).reshape(b, L, D_MODEL) + d_skip.astype(f32) * uf
    y = jax.nn.gelu(y)
    out = jnp.matmul(y, w_glu_a.astype(f32)) * jax.nn.sigmoid(jnp.matmul(y, w_glu_b.astype(f32)))
    h_last = h[:, -1]
    return out.astype(u.dtype), jnp.real(h_last), jnp.imag(h_last)


def sq_relu_mlp(x, w_up, w_down):
    h = jnp.square(jax.nn.relu(jnp.matmul(x, w_up)))
    return jnp.matmul(h, w_down)


def trunk(x, pool_hist, h0_re, h0_im, pos0, norm_mix, w_pool, pool_scale, ssm_a_re, ssm_a_im,
          ssm_log_dt, ssm_b_re, ssm_b_im, ssm_c_re, ssm_c_im, ssm_d, w_glu_a, w_glu_b,
          norm_ffn, w_up, w_down, norm_final):
    pool_new, ssm_re_new, ssm_im_new = [], [], []
    pi, si = 0, 0
    for layer in range(DEPTH):
        xn = rmsnorm(x, norm_mix[layer])
        if layer % N_MIXERS == 0:
            out, nh = pool_mixer(xn, pool_hist[pi], pos0, w_pool[pi], pool_scale[pi])
            pool_new.append(nh)
            pi += 1
        else:
            out, hr, hi = ssm_mixer(xn, h0_re[si], h0_im[si], ssm_a_re[si], ssm_a_im[si], ssm_log_dt[si],
                                    ssm_b_re[si], ssm_b_im[si], ssm_c_re[si], ssm_c_im[si], ssm_d[si],
                                    w_glu_a[si], w_glu_b[si])
            ssm_re_new.append(hr)
            ssm_im_new.append(hi)
            si += 1
        x = x + out
        x = x + sq_relu_mlp(rmsnorm(x, norm_ffn[layer]), w_up[layer], w_down[layer])
    y = rmsnorm(x, norm_final)
    return y, jnp.stack(pool_new), jnp.stack(ssm_re_new), jnp.stack(ssm_im_new)


def setup_inputs(seed: int = 0) -> dict:
    key = jax.random.key(seed)
    ks = jax.random.split(key, 24)
    f32 = jnp.float32
    G, P, C = N_SSM_GROUPS, SSM_STATE, SSM_GROUP
    nrm = lambda k, s: jax.random.normal(k, s, f32)
    x_prompt = nrm(ks[0], (BATCH, SEQ, D_MODEL))
    x_sample = nrm(ks[1], (DEC_BATCH, DEC_SEQ, D_MODEL))
    cache_pool = nrm(ks[2], (N_POOL_LAYERS, DEC_BATCH, POOL_HIST, D_MODEL))
    state_ssm_re = 0.1 * nrm(ks[3], (N_SSM_LAYERS, DEC_BATCH, G, P))
    state_ssm_im = 0.1 * nrm(ks[4], (N_SSM_LAYERS, DEC_BATCH, G, P))
    norm_mix = 1.0 + 0.01 * nrm(ks[5], (DEPTH, D_MODEL))
    w_pool = nrm(ks[6], (N_POOL_LAYERS, N_POOL_GROUPS, POOL_GROUP, POOL_GROUP)) * POOL_GROUP ** -0.5
    pool_scale = 1.0 + 0.01 * nrm(ks[7], (N_POOL_LAYERS, D_MODEL))
    ssm_a_re = -0.5 + 0.01 * nrm(ks[8], (N_SSM_LAYERS, G, P))
    ssm_a_im = math.pi * jnp.arange(P, dtype=f32)[None, None, :] + 0.01 * nrm(ks[9], (N_SSM_LAYERS, G, P))
    ssm_log_dt = jax.random.uniform(ks[10], (N_SSM_LAYERS, G), f32, math.log(DT_MIN), math.log(DT_MAX))
    ssm_b_re = nrm(ks[11], (N_SSM_LAYERS, G, P, C)) * (2.0 * C) ** -0.5
    ssm_b_im = nrm(ks[12], (N_SSM_LAYERS, G, P, C)) * (2.0 * C) ** -0.5
    ssm_c_re = nrm(ks[13], (N_SSM_LAYERS, G, C, P)) * (2.0 * P) ** -0.5
    ssm_c_im = nrm(ks[14], (N_SSM_LAYERS, G, C, P)) * (2.0 * P) ** -0.5
    ssm_d = nrm(ks[15], (N_SSM_LAYERS, D_MODEL))
    w_glu_a = nrm(ks[16], (N_SSM_LAYERS, D_MODEL, D_MODEL)) * D_MODEL ** -0.5
    w_glu_b = nrm(ks[17], (N_SSM_LAYERS, D_MODEL, D_MODEL)) * D_MODEL ** -0.5
    norm_ffn = 1.0 + 0.01 * nrm(ks[18], (DEPTH, D_MODEL))
    w_up = nrm(ks[19], (DEPTH, D_MODEL, D_FF)) * D_MODEL ** -0.5
    w_down = nrm(ks[20], (DEPTH, D_FF, D_MODEL)) * D_FF ** -0.5
    norm_final = 1.0 + 0.01 * nrm(ks[21], (D_MODEL,))
    return {"x_prompt": x_prompt, "x_sample": x_sample, "cache_pool": cache_pool,
            "state_ssm_re": state_ssm_re, "state_ssm_im": state_ssm_im,
            "norm_mix": norm_mix, "w_pool": w_pool, "pool_scale": pool_scale,
            "ssm_a_re": ssm_a_re, "ssm_a_im": ssm_a_im, "ssm_log_dt": ssm_log_dt,
            "ssm_b_re": ssm_b_re, "ssm_b_im": ssm_b_im, "ssm_c_re": ssm_c_re, "ssm_c_im": ssm_c_im,
            "ssm_d": ssm_d, "w_glu_a": w_glu_a, "w_glu_b": w_glu_b,
            "norm_ffn": norm_ffn, "w_up": w_up, "w_down": w_down, "norm_final": norm_final}


def reference(x_prompt, x_sample, cache_pool, state_ssm_re, state_ssm_im, norm_mix, w_pool, pool_scale,
              ssm_a_re, ssm_a_im, ssm_log_dt, ssm_b_re, ssm_b_im, ssm_c_re, ssm_c_im, ssm_d,
              w_glu_a, w_glu_b, norm_ffn, w_up, w_down, norm_final):
    weights = (norm_mix, w_pool, pool_scale, ssm_a_re, ssm_a_im, ssm_log_dt, ssm_b_re, ssm_b_im,
               ssm_c_re, ssm_c_im, ssm_d, w_glu_a, w_glu_b, norm_ffn, w_up, w_down, norm_final)
    b_p = x_prompt.shape[0]
    zero_hist = jnp.zeros((N_POOL_LAYERS, b_p, POOL_HIST, D_MODEL), x_prompt.dtype)
    zero_h = jnp.zeros((N_SSM_LAYERS, b_p, N_SSM_GROUPS, SSM_STATE), jnp.float32)
    y_prompt, pool_rows_p, ssm_re_p, ssm_im_p = trunk(x_prompt, zero_hist, zero_h, zero_h, 0, *weights)
    y_sample, pool_rows_s, ssm_re_s, ssm_im_s = trunk(x_sample, cache_pool, state_ssm_re, state_ssm_im,
                                                      PAST_LEN, *weights)
    return (y_prompt, y_sample, pool_rows_p, pool_rows_s, ssm_re_p, ssm_im_p, ssm_re_s, ssm_im_s)
```

```python
import functools
import math

import jax
import jax.numpy as jnp
from jax import lax
from jax.experimental import pallas as pl
from jax.experimental.pallas import tpu as pltpu

F32 = jnp.float32
BF16 = jnp.bfloat16

EPS = 1e-6
PAST_LEN = 1024
POOL_WINDOWS = (2, 4, 8, 16)
POOL_HIST = max(POOL_WINDOWS) - 1
HIST_PAD = POOL_HIST + 1
SSM_GROUP = 16
SSM_STATE = 64
SSM_CHUNK = 16
LANES = 128
GROUPS_PER_TILE = LANES // SSM_GROUP
STATE_LANES = GROUPS_PER_TILE * SSM_STATE
VMEM_LIMIT = 56 * 1024 * 1024


def _cparams(sem):
    return pltpu.CompilerParams(dimension_semantics=sem, vmem_limit_bytes=VMEM_LIMIT)


def _rms(x, g):
    return x * lax.rsqrt(jnp.mean(x * x, axis=-1, keepdims=True) + EPS) * g


def _rmsnorm_kernel(x_ref, g_ref, o_ref):
    o_ref[...] = _rms(x_ref[...], g_ref[...]).astype(o_ref.dtype)


def _rmsnorm(x, g, out_dtype, tm, row0=0, rows=None):
    m, d = x.shape
    rows = m - row0 if rows is None else rows
    assert rows % tm == 0 and row0 % tm == 0
    off = row0 // tm
    return pl.pallas_call(
        _rmsnorm_kernel,
        grid=(rows // tm,),
        in_specs=[pl.BlockSpec((tm, d), lambda i: (i + off, 0)),
                  pl.BlockSpec((1, d), lambda i: (0, 0))],
        out_specs=pl.BlockSpec((tm, d), lambda i: (i, 0)),
        out_shape=jax.ShapeDtypeStruct((rows, d), out_dtype),
        compiler_params=_cparams(("parallel",)),
        name="rmsnorm",
    )(x, g.reshape(1, d))


def _pool_prompt_kernel(x_ref, gm_ref, w_ref, sc_ref, gf_ref, x1_ref, xnf_ref, hist_ref, ext_ref, *, tm):
    i = pl.program_id(0)
    d = x_ref.shape[1]
    pg = d // len(POOL_WINDOWS)

    @pl.when(i == 0)
    def _():
        ext_ref[0:HIST_PAD, :] = jnp.zeros((HIST_PAD, d), F32)

    x = x_ref[...]
    ext_ref[HIST_PAD:, :] = _rms(x, gm_ref[...])

    pos = i * tm + lax.broadcasted_iota(jnp.int32, (tm, 1), 0)
    ss = jnp.zeros((tm, 1), F32)
    for gi, w in enumerate(POOL_WINDOWS):
        sl = slice(gi * pg, (gi + 1) * pg)
        a = ext_ref[:, sl]
        s = a
        span = 1
        while span < w:
            s = s + pltpu.roll(s, span, 0)
            span *= 2
        cnt = jnp.minimum(w, pos + 1).astype(F32)
        pooled = s[HIST_PAD:, :] / cnt - a[HIST_PAD:, :]
        mixed = jnp.dot(pooled.astype(BF16), w_ref[gi], preferred_element_type=F32)
        x1 = x[:, sl] + mixed * sc_ref[:, sl]
        x1_ref[:, sl] = x1
        ss = ss + jnp.sum(x1 * x1, axis=-1, keepdims=True)

    inv = lax.rsqrt(ss / d + EPS)
    xnf_ref[...] = (x1_ref[...] * inv * gf_ref[...]).astype(xnf_ref.dtype)
    hist_ref[...] = ext_ref[tm:, :]
    ext_ref[0:HIST_PAD, :] = ext_ref[tm:, :]


def _pool_prompt(x, g_mix, w_pool_bf, scale, g_ffn, tm):
    l, d = x.shape
    ng, pg, _ = w_pool_bf.shape
    return pl.pallas_call(
        functools.partial(_pool_prompt_kernel, tm=tm),
        grid=(l // tm,),
        in_specs=[pl.BlockSpec((tm, d), lambda i: (i, 0)),
                  pl.BlockSpec((1, d), lambda i: (0, 0)),
                  pl.BlockSpec((ng, pg, pg), lambda i: (0, 0, 0)),
                  pl.BlockSpec((1, d), lambda i: (0, 0)),
                  pl.BlockSpec((1, d), lambda i: (0, 0))],
        out_specs=[pl.BlockSpec((tm, d), lambda i: (i, 0)),
                   pl.BlockSpec((tm, d), lambda i: (i, 0)),
                   pl.BlockSpec((HIST_PAD, d), lambda i: (0, 0))],
        out_shape=[jax.ShapeDtypeStruct((l, d), F32),
                   jax.ShapeDtypeStruct((l, d), BF16),
                   jax.ShapeDtypeStruct((HIST_PAD, d), F32)],
        scratch_shapes=[pltpu.VMEM((HIST_PAD + tm, d), F32)],
        compiler_params=_cparams(("arbitrary",)),
        name="pool_prompt",
    )(x, g_mix.reshape(1, d), w_pool_bf, scale.reshape(1, d), g_ffn.reshape(1, d))


def _pool_sample_kernel(x_ref, h_ref, gm_ref, w_ref, sc_ref, gf_ref, x1_ref, xnf_ref, xn_ref, *, seq):
    m, d = x_ref.shape
    pg = d // len(POOL_WINDOWS)
    x = x_ref[...]
    xn_ref[...] = _rms(x, gm_ref[...])
    t_in_seq = lax.broadcasted_iota(jnp.int32, (m, 1), 0) % seq
    ss = jnp.zeros((m, 1), F32)
    for gi, w in enumerate(POOL_WINDOWS):
        sl = slice(gi * pg, (gi + 1) * pg)
        a = xn_ref[:, sl]
        h = h_ref[:, sl]
        s = a
        for lag in range(1, w):
            cur = pltpu.roll(a, lag, 0)
            old = pltpu.roll(h, (m - HIST_PAD + lag) % m, 0)
            s = s + jnp.where(t_in_seq >= lag, cur, old)
        pooled = s / float(w) - a
        mixed = jnp.dot(pooled.astype(BF16), w_ref[gi], preferred_element_type=F32)
        x1 = x[:, sl] + mixed * sc_ref[:, sl]
        x1_ref[:, sl] = x1
        ss = ss + jnp.sum(x1 * x1, axis=-1, keepdims=True)
    inv = lax.rsqrt(ss / d + EPS)
    xnf_ref[...] = (x1_ref[...] * inv * gf_ref[...]).astype(xnf_ref.dtype)


def _pool_sample(x, hist, g_mix, w_pool_bf, scale, g_ffn, seq):
    m, d = x.shape
    ng, pg, _ = w_pool_bf.shape
    full = lambda shape: pl.BlockSpec(shape, lambda i: (0,) * len(shape))
    return pl.pallas_call(
        functools.partial(_pool_sample_kernel, seq=seq),
        grid=(1,),
        in_specs=[full((m, d)), full((m, d)), full((1, d)), full((ng, pg, pg)), full((1, d)), full((1, d))],
        out_specs=[full((m, d)), full((m, d)), full((m, d))],
        out_shape=[jax.ShapeDtypeStruct((m, d), F32),
                   jax.ShapeDtypeStruct((m, d), BF16),
                   jax.ShapeDtypeStruct((m, d), F32)],
        compiler_params=_cparams(("arbitrary",)),
        name="pool_sample",
    )(x, hist, g_mix.reshape(1, d), w_pool_bf, scale.reshape(1, d), g_ffn.reshape(1, d))


def _up_kernel(x_ref, w_ref, o_ref):
    acc = jnp.dot(x_ref[...], w_ref[...].astype(BF16), preferred_element_type=F32)
    h = jnp.maximum(acc, 0.0)
    o_ref[...] = (h * h).astype(o_ref.dtype)


def _mlp_up(xn, w, tm, tn):
    m, k = xn.shape
    n = w.shape[1]
    return pl.pallas_call(
        _up_kernel,
        grid=(m // tm, n // tn),
        in_specs=[pl.BlockSpec((tm, k), lambda i, j: (i, 0), pipeline_mode=pl.Buffered(1)),
                  pl.BlockSpec((k, tn), lambda i, j: (0, j))],
        out_specs=pl.BlockSpec((tm, tn), lambda i, j: (i, j)),
        out_shape=jax.ShapeDtypeStruct((m, n), BF16),
        compiler_params=_cparams(("parallel", "arbitrary")),
        name="mlp_up",
    )(xn, w)


def _down_kernel(h_ref, w_ref, r_ref, o_ref):
    @pl.when(pl.program_id(2) == 0)
    def _():
        o_ref[...] = r_ref[...]

    o_ref[...] += jnp.dot(h_ref[...], w_ref[...].astype(BF16), preferred_element_type=F32)


def _mlp_down(h, w, resid, tm, tn, tk):
    m, k = h.shape
    n = w.shape[1]
    return pl.pallas_call(
        _down_kernel,
        grid=(m // tm, n // tn, k // tk),
        in_specs=[pl.BlockSpec((tm, tk), lambda i, j, kk: (i, kk)),
                  pl.BlockSpec((tk, tn), lambda i, j, kk: (kk, j)),
                  pl.BlockSpec((tm, tn), lambda i, j, kk: (i, j), pipeline_mode=pl.Buffered(1))],
        out_specs=pl.BlockSpec((tm, tn), lambda i, j, kk: (i, j)),
        out_shape=jax.ShapeDtypeStruct((m, n), F32),
        compiler_params=_cparams(("parallel", "parallel", "arbitrary")),
        name="mlp_down",
    )(h, w, resid)


def _glu_kernel(y_ref, wa_ref, wb_ref, r_ref, o_ref):
    y = y_ref[...]
    a = jnp.dot(y, wa_ref[...].astype(BF16), preferred_element_type=F32)
    b = jnp.dot(y, wb_ref[...].astype(BF16), preferred_element_type=F32)
    o_ref[...] = r_ref[...] + a * jax.nn.sigmoid(b)


def _glu(y, wa, wb, resid, tm, tn):
    m, k = y.shape
    n = wa.shape[1]
    return pl.pallas_call(
        _glu_kernel,
        grid=(m // tm, n // tn),
        in_specs=[pl.BlockSpec((tm, k), lambda i, j: (i, 0), pipeline_mode=pl.Buffered(1)),
                  pl.BlockSpec((k, tn), lambda i, j: (0, j)),
                  pl.BlockSpec((k, tn), lambda i, j: (0, j)),
                  pl.BlockSpec((tm, tn), lambda i, j: (i, j))],
        out_specs=pl.BlockSpec((tm, tn), lambda i, j: (i, j)),
        out_shape=jax.ShapeDtypeStruct((m, n), F32),
        compiler_params=_cparams(("parallel", "arbitrary")),
        name="glu",
    )(y, wa, wb, resid)


def _ssm_tables(a_re, a_im, log_dt, b_re, b_im, c_re, c_im):
    t = SSM_CHUNK
    g, p = a_re.shape
    c = b_re.shape[-1]
    tiles = g // GROUPS_PER_TILE
    lam = lax.complex(a_re, a_im)
    dt = jnp.exp(log_dt)[:, None]
    a_bar = jnp.exp(lam * dt)
    b_bar = ((a_bar - 1.0) / lam)[..., None] * lax.complex(b_re, b_im)
    cc = lax.complex(c_re, c_im)
    steps = jnp.arange(t + 1, dtype=F32)[:, None, None]
    apow = jnp.exp(lam * dt * steps)
    eye = jnp.eye(GROUPS_PER_TILE, dtype=F32)

    bst = apow[t - 1::-1][:, :, :, None] * b_bar[None]
    bst = jnp.stack([jnp.real(bst), jnp.imag(bst)], axis=0)
    bst = bst.reshape(2, t, tiles, GROUPS_PER_TILE, p, c)
    bexp = jnp.einsum('rjighc,gk->ijgcrkh', bst, eye)
    bexp = bexp.reshape(tiles, t * LANES, 2 * STATE_LANES).astype(BF16)

    ca = cc[None] * apow[1:][:, :, None, :]
    ca = jnp.stack([jnp.real(ca), -jnp.imag(ca)], axis=0)
    ca = ca.reshape(2, t, tiles, GROUPS_PER_TILE, c, p)
    caexp = jnp.einsum('rtigch,gk->irghtkc', ca, eye)
    caexp = caexp.reshape(tiles, 2 * STATE_LANES, t * LANES).astype(BF16)

    kern = jnp.real(jnp.einsum('gep,ngp,gpc->ngce', cc, apow[:t], b_bar))
    kern = kern[::-1].reshape(t, tiles, GROUPS_PER_TILE, c, c)
    krev = jnp.einsum('nigce,gk->ingcke', kern, eye)
    krev = krev.reshape(tiles, t * LANES, LANES).astype(BF16)

    at = apow[t].reshape(tiles, STATE_LANES)
    apow_t = jnp.concatenate([jnp.real(at), jnp.imag(at)], axis=-1).reshape(tiles, 1, 2 * STATE_LANES)
    return bexp, caexp, krev, apow_t


def _gelu_tanh(x):
    return 0.5 * x * (1.0 + jnp.tanh(math.sqrt(2.0 / math.pi) * (x + 0.044715 * x * x * x)))


def _ssm_kernel(x_ref, b_ref, ca_ref, k_ref, at_ref, h0_ref, d_ref, y_ref, hf_ref,
                xs_ref, ys_ref, g_ref, hp_ref, *, n_prompt_chunks):
    t = SSM_CHUNK
    nch = x_ref.shape[0] // t
    ns = nch - n_prompt_chunks
    sl = STATE_LANES
    xs_ref[...] = x_ref[...].astype(F32)
    us = [xs_ref[pl.ds(j, nch, stride=t), :] for j in range(t)]
    un = jnp.concatenate([u.astype(BF16) for u in us], axis=1)
    g_ref[...] = jnp.dot(un, b_ref[0], preferred_element_type=F32)

    at = at_ref[0]
    a_re, a_im = at[:, :sl], at[:, sl:]

    def step(k, h):
        hp_ref[pl.ds(k, 1), :] = h
        g = g_ref[pl.ds(k, 1), :]
        h_re, h_im = h[:, :sl], h[:, sl:]
        n_re = a_re * h_re - a_im * h_im + g[:, :sl]
        n_im = a_re * h_im + a_im * h_re + g[:, sl:]
        return jnp.concatenate([n_re, n_im], axis=1)

    h_end = lax.fori_loop(0, n_prompt_chunks, step, jnp.zeros((1, 2 * sl), F32))

    h0 = h0_ref[0]
    hp_ref[n_prompt_chunks:, :] = h0
    gs = g_ref[n_prompt_chunks:, :]
    s_re = a_re * h0[:, :sl] - a_im * h0[:, sl:] + gs[:, :sl]
    s_im = a_re * h0[:, sl:] + a_im * h0[:, :sl] + gs[:, sl:]
    hf_ref[0, 0:ns, :] = jnp.concatenate([s_re, s_im], axis=1)
    hf_ref[0, ns:, :] = jnp.broadcast_to(h_end, (hf_ref.shape[1] - ns, 2 * sl))

    y_in = jnp.dot(hp_ref[...].astype(BF16), ca_ref[0], preferred_element_type=F32)
    dsk = d_ref[...]
    for j in range(t):
        y = jnp.dot(un[:, :(j + 1) * LANES], k_ref[0, (t - 1 - j) * LANES:, :], preferred_element_type=F32)
        y = y + y_in[:, j * LANES:(j + 1) * LANES] + dsk * us[j]
        ys_ref[pl.ds(j, nch, stride=t), :] = _gelu_tanh(y)
    y_ref[...] = ys_ref[...].astype(y_ref.dtype)


def _ssm(xn, tables, h0, d_skip, n_prompt_chunks):
    bexp, caexp, krev, apow_t = tables
    m, d = xn.shape
    tiles = d // LANES
    t = SSM_CHUNK
    nch = m // t
    ns = nch - n_prompt_chunks
    hf_rows = 2 * ns
    tl = t * LANES
    s2 = 2 * STATE_LANES
    return pl.pallas_call(
        functools.partial(_ssm_kernel, n_prompt_chunks=n_prompt_chunks),
        grid=(tiles,),
        in_specs=[pl.BlockSpec((m, LANES), lambda i: (0, i)),
                  pl.BlockSpec((1, tl, s2), lambda i: (i, 0, 0)),
                  pl.BlockSpec((1, s2, tl), lambda i: (i, 0, 0)),
                  pl.BlockSpec((1, tl, LANES), lambda i: (i, 0, 0)),
                  pl.BlockSpec((1, 1, s2), lambda i: (i, 0, 0)),
                  pl.BlockSpec((1, ns, s2), lambda i: (i, 0, 0)),
                  pl.BlockSpec((1, LANES), lambda i: (0, i))],
        out_specs=[pl.BlockSpec((m, LANES), lambda i: (0, i)),
                   pl.BlockSpec((1, hf_rows, s2), lambda i: (i, 0, 0))],
        out_shape=[jax.ShapeDtypeStruct((m, d), BF16),
                   jax.ShapeDtypeStruct((tiles, hf_rows, s2), F32)],
        scratch_shapes=[pltpu.VMEM((m, LANES), F32),
                        pltpu.VMEM((m, LANES), F32),
                        pltpu.VMEM((nch, s2), F32),
                        pltpu.VMEM((nch, s2), F32)],
        compiler_params=_cparams(("parallel",)),
        name="ssm",
    )(xn, bexp, caexp, krev, apow_t, h0, d_skip.reshape(1, d))


def _state_to_tiles(h_re, h_im):
    b, g, p = h_re.shape
    tiles = g // GROUPS_PER_TILE
    re = h_re.reshape(b, tiles, STATE_LANES).transpose(1, 0, 2)
    im = h_im.reshape(b, tiles, STATE_LANES).transpose(1, 0, 2)
    return jnp.concatenate([re, im], axis=-1)


def _tiles_to_state(h):
    tiles, b, _ = h.shape
    re = h[..., :STATE_LANES].transpose(1, 0, 2).reshape(b, tiles * GROUPS_PER_TILE, SSM_STATE)
    im = h[..., STATE_LANES:].transpose(1, 0, 2).reshape(b, tiles * GROUPS_PER_TILE, SSM_STATE)
    return re, im


ROW_TILE = 1664
NORM_TILE = 416
POOL_TILE = 256


def kernel(x_prompt, x_sample, cache_pool, state_ssm_re, state_ssm_im, norm_mix, w_pool, pool_scale, ssm_a_re, ssm_a_im, ssm_log_dt, ssm_b_re, ssm_b_im, ssm_c_re, ssm_c_im, ssm_d, w_glu_a, w_glu_b, norm_ffn, w_up, w_down, norm_final):
    bp, lp, d = x_prompt.shape
    bs, ls, _ = x_sample.shape
    assert bp == 1 and ls == SSM_CHUNK and ls == HIST_PAD and lp % SSM_CHUNK == 0
    ms = bs * ls
    m = lp + ms

    w_pool_bf = w_pool[0].astype(BF16)
    x1_p, xnf_p, hist_p = _pool_prompt(x_prompt[0], norm_mix[0], w_pool_bf, pool_scale[0], norm_ffn[0], POOL_TILE)
    hist_s = jnp.pad(cache_pool[0], ((0, 0), (1, 0), (0, 0))).reshape(ms, d)
    x1_s, xnf_s, xn_s = _pool_sample(x_sample.reshape(ms, d), hist_s, norm_mix[0], w_pool_bf, pool_scale[0],
                                     norm_ffn[0], ls)
    x = jnp.concatenate([x1_p, x1_s], axis=0)
    xn = jnp.concatenate([xnf_p, xnf_s], axis=0)
    pool_rows_p = hist_p[1:].reshape(1, 1, POOL_HIST, d)
    pool_rows_s = xn_s.reshape(bs, ls, d)[:, 1:].reshape(1, bs, POOL_HIST, d)

    h = _mlp_up(xn, w_up[0], ROW_TILE, 512)
    x = _mlp_down(h, w_down[0], x, ROW_TILE, 1024, 1024)

    xn = _rmsnorm(x, norm_mix[1], BF16, NORM_TILE)
    tables = _ssm_tables(ssm_a_re[0], ssm_a_im[0], ssm_log_dt[0], ssm_b_re[0], ssm_b_im[0],
                         ssm_c_re[0], ssm_c_im[0])
    h0 = _state_to_tiles(state_ssm_re[0], state_ssm_im[0])
    y, hf = _ssm(xn, tables, h0, ssm_d[0], lp // SSM_CHUNK)
    x = _glu(y, w_glu_a[0], w_glu_b[0], x, ROW_TILE, 256)
    re_s, im_s = _tiles_to_state(hf[:, :bs])
    re_p, im_p = _tiles_to_state(hf[:, bs:bs + 1])

    xn = _rmsnorm(x, norm_ffn[1], BF16, NORM_TILE)
    h = _mlp_up(xn, w_up[1], ROW_TILE, 512)
    x = _mlp_down(h, w_down[1], x, ROW_TILE, 1024, 1024)

    y_p = _rmsnorm(x, norm_final, F32, 512, 0, lp).reshape(1, lp, d)
    y_s = _rmsnorm(x, norm_final, F32, ms, lp, ms).reshape(bs, ls, d)
    return (y_p, y_s, pool_rows_p, pool_rows_s, re_p[None], im_p[None], re_s[None], im_s[None])
```

```python
import functools
import math

import jax
import jax.numpy as jnp
from jax import lax
from jax.experimental import pallas as pl
from jax.experimental.pallas import tpu as pltpu

F32 = jnp.float32
BF16 = jnp.bfloat16

EPS = 1e-6
PAST_LEN = 1024
POOL_WINDOWS = (2, 4, 8, 16)
POOL_HIST = max(POOL_WINDOWS) - 1
HIST_PAD = POOL_HIST + 1
SSM_GROUP = 16
SSM_STATE = 64
SSM_CHUNK = 16
LANES = 128
SUBLANES = 8
GROUPS_PER_TILE = LANES // SSM_GROUP
STATE_LANES = GROUPS_PER_TILE * SSM_STATE
VMEM_LIMIT = 56 * 1024 * 1024


def _cparams(sem):
    return pltpu.CompilerParams(dimension_semantics=sem, vmem_limit_bytes=VMEM_LIMIT)


def _rms(x, g):
    return x * lax.rsqrt(jnp.mean(x * x, axis=-1, keepdims=True) + EPS) * g


def _rmsnorm_kernel(x_ref, g_ref, o_ref):
    o_ref[...] = _rms(x_ref[...], g_ref[...]).astype(o_ref.dtype)


def _rmsnorm(x, g, out_dtype, tm, row0=0, rows=None):
    m, d = x.shape
    rows = m - row0 if rows is None else rows
    assert rows % tm == 0 and row0 % tm == 0
    off = row0 // tm
    return pl.pallas_call(
        _rmsnorm_kernel,
        grid=(rows // tm,),
        in_specs=[pl.BlockSpec((tm, d), lambda i: (i + off, 0)),
                  pl.BlockSpec((1, d), lambda i: (0, 0))],
        out_specs=pl.BlockSpec((tm, d), lambda i: (i, 0)),
        out_shape=jax.ShapeDtypeStruct((rows, d), out_dtype),
        compiler_params=_cparams(("parallel",)),
        name="rmsnorm",
    )(x, g.reshape(1, d))


def _pool_prompt_kernel(x_ref, gm_ref, w_ref, sc_ref, gf_ref, x1_ref, xnf_ref, hist_ref, ext_ref, *, tm):
    i = pl.program_id(0)
    d = x_ref.shape[1]
    pg = d // len(POOL_WINDOWS)

    @pl.when(i == 0)
    def _():
        ext_ref[0:HIST_PAD, :] = jnp.zeros((HIST_PAD, d), F32)

    x = x_ref[...]
    ext_ref[HIST_PAD:, :] = _rms(x, gm_ref[...])

    pos = i * tm + lax.broadcasted_iota(jnp.int32, (tm, 1), 0)
    ss = jnp.zeros((tm, 1), F32)
    for gi, w in enumerate(POOL_WINDOWS):
        sl = slice(gi * pg, (gi + 1) * pg)
        a = ext_ref[:, sl]
        s = a
        span = 1
        while span < w:
            s = s + pltpu.roll(s, span, 0)
            span *= 2
        cnt = jnp.minimum(w, pos + 1).astype(F32)
        pooled = s[HIST_PAD:, :] / cnt - a[HIST_PAD:, :]
        mixed = jnp.dot(pooled.astype(BF16), w_ref[gi], preferred_element_type=F32)
        x1 = x[:, sl] + mixed * sc_ref[:, sl]
        x1_ref[:, sl] = x1
        ss = ss + jnp.sum(x1 * x1, axis=-1, keepdims=True)

    inv = lax.rsqrt(ss / d + EPS)
    xnf_ref[...] = (x1_ref[...] * inv * gf_ref[...]).astype(xnf_ref.dtype)
    hist_ref[...] = ext_ref[tm:, :]
    ext_ref[0:HIST_PAD, :] = ext_ref[tm:, :]


def _pool_prompt(x, g_mix, w_pool_bf, scale, g_ffn, tm, m_total):
    l, d = x.shape
    ng, pg, _ = w_pool_bf.shape
    return pl.pallas_call(
        functools.partial(_pool_prompt_kernel, tm=tm),
        grid=(l // tm,),
        in_specs=[pl.BlockSpec((tm, d), lambda i: (i, 0)),
                  pl.BlockSpec((1, d), lambda i: (0, 0)),
                  pl.BlockSpec((ng, pg, pg), lambda i: (0, 0, 0)),
                  pl.BlockSpec((1, d), lambda i: (0, 0)),
                  pl.BlockSpec((1, d), lambda i: (0, 0))],
        out_specs=[pl.BlockSpec((tm, d), lambda i: (i, 0)),
                   pl.BlockSpec((tm, d), lambda i: (i, 0)),
                   pl.BlockSpec((HIST_PAD, d), lambda i: (0, 0))],
        out_shape=[jax.ShapeDtypeStruct((m_total, d), F32),
                   jax.ShapeDtypeStruct((m_total, d), BF16),
                   jax.ShapeDtypeStruct((HIST_PAD, d), F32)],
        scratch_shapes=[pltpu.VMEM((HIST_PAD + tm, d), F32)],
        compiler_params=_cparams(("arbitrary",)),
        name="pool_prompt",
    )(x, g_mix.reshape(1, d), w_pool_bf, scale.reshape(1, d), g_ffn.reshape(1, d))


def _pool_sample_kernel(x_ref, h_ref, gm_ref, w_ref, sc_ref, gf_ref, x1_in, xnf_in, x1_ref, xnf_ref, xn_ref, *, seq):
    del x1_in, xnf_in
    m, d = x_ref.shape
    pg = d // len(POOL_WINDOWS)
    x = x_ref[...]
    xn_ref[...] = _rms(x, gm_ref[...])
    t_in_seq = lax.broadcasted_iota(jnp.int32, (m, 1), 0) % seq
    ss = jnp.zeros((m, 1), F32)
    for gi, w in enumerate(POOL_WINDOWS):
        sl = slice(gi * pg, (gi + 1) * pg)
        a = xn_ref[:, sl]
        h = h_ref[:, sl]
        s = a
        for lag in range(1, w):
            cur = pltpu.roll(a, lag, 0)
            old = pltpu.roll(h, (m - HIST_PAD + lag) % m, 0)
            s = s + jnp.where(t_in_seq >= lag, cur, old)
        pooled = s / float(w) - a
        mixed = jnp.dot(pooled.astype(BF16), w_ref[gi], preferred_element_type=F32)
        x1 = x[:, sl] + mixed * sc_ref[:, sl]
        x1_ref[:, sl] = x1
        ss = ss + jnp.sum(x1 * x1, axis=-1, keepdims=True)
    inv = lax.rsqrt(ss / d + EPS)
    xnf_ref[...] = (x1_ref[...] * inv * gf_ref[...]).astype(xnf_ref.dtype)


def _pool_sample(x, hist, g_mix, w_pool_bf, scale, g_ffn, seq, x1_all, xnf_all):
    m, d = x.shape
    m_total = x1_all.shape[0]
    assert (m_total - m) % m == 0
    last = (m_total - m) // m
    ng, pg, _ = w_pool_bf.shape
    full = lambda shape: pl.BlockSpec(shape, lambda i: (0,) * len(shape))
    anyspec = pl.BlockSpec(memory_space=pl.ANY)
    return pl.pallas_call(
        functools.partial(_pool_sample_kernel, seq=seq),
        grid=(1,),
        in_specs=[full((m, d)), full((m, d)), full((1, d)), full((ng, pg, pg)), full((1, d)), full((1, d)),
                  anyspec, anyspec],
        out_specs=[pl.BlockSpec((m, d), lambda i: (last, 0)),
                   pl.BlockSpec((m, d), lambda i: (last, 0)),
                   full((m, d))],
        out_shape=[jax.ShapeDtypeStruct((m_total, d), F32),
                   jax.ShapeDtypeStruct((m_total, d), BF16),
                   jax.ShapeDtypeStruct((m, d), F32)],
        input_output_aliases={6: 0, 7: 1},
        compiler_params=_cparams(("arbitrary",)),
        name="pool_sample",
    )(x, hist, g_mix.reshape(1, d), w_pool_bf, scale.reshape(1, d), g_ffn.reshape(1, d), x1_all, xnf_all)


def _up_kernel(x_ref, w_ref, o_ref):
    acc = jnp.dot(x_ref[...], w_ref[...].astype(BF16), preferred_element_type=F32)
    h = jnp.maximum(acc, 0.0)
    o_ref[...] = (h * h).astype(o_ref.dtype)


def _mlp_up(xn, w, tm, tn):
    m, k = xn.shape
    n = w.shape[1]
    return pl.pallas_call(
        _up_kernel,
        grid=(m // tm, n // tn),
        in_specs=[pl.BlockSpec((tm, k), lambda i, j: (i, 0), pipeline_mode=pl.Buffered(1)),
                  pl.BlockSpec((k, tn), lambda i, j: (0, j))],
        out_specs=pl.BlockSpec((tm, tn), lambda i, j: (i, j)),
        out_shape=jax.ShapeDtypeStruct((m, n), BF16),
        compiler_params=_cparams(("parallel", "arbitrary")),
        name="mlp_up",
    )(xn, w)


def _down_kernel(h_ref, w_ref, r_ref, o_ref):
    @pl.when(pl.program_id(2) == 0)
    def _():
        o_ref[...] = r_ref[...]

    o_ref[...] += jnp.dot(h_ref[...], w_ref[...].astype(BF16), preferred_element_type=F32)


def _mlp_down(h, w, resid, tm, tn, tk):
    m, k = h.shape
    n = w.shape[1]
    return pl.pallas_call(
        _down_kernel,
        grid=(m // tm, n // tn, k // tk),
        in_specs=[pl.BlockSpec((tm, tk), lambda i, j, kk: (i, kk)),
                  pl.BlockSpec((tk, tn), lambda i, j, kk: (kk, j)),
                  pl.BlockSpec((tm, tn), lambda i, j, kk: (i, j), pipeline_mode=pl.Buffered(1))],
        out_specs=pl.BlockSpec((tm, tn), lambda i, j, kk: (i, j)),
        out_shape=jax.ShapeDtypeStruct((m, n), F32),
        compiler_params=_cparams(("parallel", "parallel", "arbitrary")),
        name="mlp_down",
    )(h, w, resid)


def _glu_kernel(y_ref, wa_ref, wb_ref, r_ref, o_ref):
    y = y_ref[...]
    a = jnp.dot(y, wa_ref[...].astype(BF16), preferred_element_type=F32)
    b = jnp.dot(y, wb_ref[...].astype(BF16), preferred_element_type=F32)
    o_ref[...] = r_ref[...] + a * jax.nn.sigmoid(b)


def _glu(y, wa, wb, resid, tm, tn):
    m, k = y.shape
    n = wa.shape[1]
    return pl.pallas_call(
        _glu_kernel,
        grid=(m // tm, n // tn),
        in_specs=[pl.BlockSpec((tm, k), lambda i, j: (i, 0), pipeline_mode=pl.Buffered(1)),
                  pl.BlockSpec((k, tn), lambda i, j: (0, j)),
                  pl.BlockSpec((k, tn), lambda i, j: (0, j)),
                  pl.BlockSpec((tm, tn), lambda i, j: (i, j))],
        out_specs=pl.BlockSpec((tm, tn), lambda i, j: (i, j)),
        out_shape=jax.ShapeDtypeStruct((m, n), F32),
        compiler_params=_cparams(("parallel", "arbitrary")),
        name="glu",
    )(y, wa, wb, resid)


def _cmul(a_re, a_im, b_re, b_im):
    return a_re * b_re - a_im * b_im, a_re * b_im + a_im * b_re


def _ssm_prep_kernel(are_ref, aim_ref, ldt_ref, brp_ref, bpr_ref, ca_ref, cb_ref, q_ref, cat_ref, pt_ref):
    t = SSM_CHUNK
    rows = are_ref.shape[0]
    l_re, l_im = are_ref[...], aim_ref[...]
    dt = jnp.exp(ldt_ref[...])
    mag = jnp.exp(l_re * dt)
    a_re, a_im = mag * jnp.cos(l_im * dt), mag * jnp.sin(l_im * dt)
    den = l_re * l_re + l_im * l_im
    n_re, n_im = a_re - 1.0, a_im
    k_re = (n_re * l_re + n_im * l_im) / den
    k_im = (n_im * l_re - n_re * l_im) / den
    b_rp, b_pr = brp_ref[...], bpr_ref[...]
    bb_rp = k_re * b_rp + k_im * b_pr
    bb_pr = k_re * b_pr - k_im * b_rp
    c_a, c_b = ca_ref[...], cb_ref[...]
    p_re, p_im = jnp.ones_like(a_re), jnp.zeros_like(a_re)
    for n in range(t):
        j = t - 1 - n
        q_ref[0, j * rows:(j + 1) * rows, :] = (p_re * bb_rp + p_im * bb_pr).astype(q_ref.dtype)
        p_re, p_im = _cmul(p_re, p_im, a_re, a_im)
        cat_ref[0, n * rows:(n + 1) * rows, :] = (p_re * c_a + p_im * c_b).astype(cat_ref.dtype)
    half = lax.broadcasted_iota(jnp.int32, p_re.shape, 1) < (p_re.shape[1] // 2)
    pt_ref[...] = jnp.where(half, p_re, p_im)


def _ssm_prep(a_re, a_im, log_dt, b_re, b_im, c_re, c_im):
    g, p = a_re.shape
    c = b_re.shape[-1]
    t = SSM_CHUNK
    tiles = g // GROUPS_PER_TILE
    rows = g * c
    dup = lambda v: jnp.concatenate([v, v], axis=-1)
    per_row = lambda v: dup(jnp.repeat(v, c, axis=0))
    bt_re = b_re.transpose(0, 2, 1).reshape(rows, p)
    bt_im = b_im.transpose(0, 2, 1).reshape(rows, p)
    cr, ci = c_re.reshape(rows, p), c_im.reshape(rows, p)
    args = (per_row(a_re), per_row(a_im), per_row(jnp.broadcast_to(log_dt[:, None], (g, p))),
            jnp.concatenate([bt_re, bt_im], axis=-1), jnp.concatenate([-bt_im, bt_re], axis=-1),
            jnp.concatenate([cr, -ci], axis=-1), jnp.concatenate([-ci, -cr], axis=-1))
    blk = pl.BlockSpec((LANES, 2 * p), lambda i: (i, 0))
    tab = pl.BlockSpec((1, t * LANES, 2 * p), lambda i: (i, 0, 0))
    q, cat, pt = pl.pallas_call(
        _ssm_prep_kernel,
        grid=(tiles,),
        in_specs=[blk] * 7,
        out_specs=[tab, tab, blk],
        out_shape=[jax.ShapeDtypeStruct((tiles, t * LANES, 2 * p), BF16),
                   jax.ShapeDtypeStruct((tiles, t * LANES, 2 * p), BF16),
                   jax.ShapeDtypeStruct((rows, 2 * p), F32)],
        compiler_params=_cparams(("parallel",)),
        name="ssm_prep",
    )(*args)
    pt = pt.reshape(tiles, GROUPS_PER_TILE, c, 2, p)[:, :, 0]
    a_t = pt.transpose(0, 2, 1, 3).reshape(tiles, 1, 2 * STATE_LANES)
    ca_rows = args[5].astype(BF16)
    return q, cat, ca_rows, a_t


def _gelu_tanh(x):
    return 0.5 * x * (1.0 + jnp.tanh(math.sqrt(2.0 / math.pi) * (x + 0.044715 * x * x * x)))


def _ssm_kernel(x_ref, q_ref, cat_ref, ca_ref, at_ref, h0_ref, d_ref, y_ref, hf_ref,
                xs_ref, ys_ref, g_ref, hp_ref, bexp_ref, cexp_ref, *, n_prompt_chunks):
    t = SSM_CHUNK
    nch = x_ref.shape[0] // t
    ns = nch - n_prompt_chunks
    sl = STATE_LANES
    ii = lambda shape, dim: lax.broadcasted_iota(jnp.int32, shape, dim)

    r2, c2 = ii((LANES, 2 * sl), 0), ii((LANES, 2 * sl), 1)
    rep = ((r2 // SSM_STATE == c2 // sl) & (r2 % SSM_STATE == c2 % SSM_STATE)).astype(BF16)
    own = (r2 // SSM_GROUP) == (c2 // SSM_STATE) % GROUPS_PER_TILE
    for j in range(t):
        rows = slice(j * LANES, (j + 1) * LANES)
        e = jnp.dot(q_ref[0, rows, :], rep, preferred_element_type=F32)
        bexp_ref[rows, :] = jnp.where(own, e, 0.0).astype(BF16)
        e = jnp.dot(cat_ref[0, rows, :], rep, preferred_element_type=F32)
        cexp_ref[rows, :] = jnp.where(own, e, 0.0).astype(BF16)
    kr = lax.dot_general(q_ref[0], ca_ref[...], (((1,), (1,)), ((), ())), preferred_element_type=F32)
    same = (ii((LANES, LANES), 0) // SSM_GROUP) == (ii((LANES, LANES), 1) // SSM_GROUP)
    krev = [jnp.where(same, kr[j * LANES:(j + 1) * LANES, :], 0.0).astype(BF16) for j in range(t)]

    xs_ref[...] = x_ref[...].astype(F32)
    us = [xs_ref[pl.ds(j, nch, stride=t), :] for j in range(t)]
    un = jnp.concatenate([u.astype(BF16) for u in us], axis=1)
    g_ref[...] = jnp.dot(un, bexp_ref[...], preferred_element_type=F32)

    at = at_ref[0]
    a_re, a_im = at[:, :sl], at[:, sl:]
    rid = ii((SUBLANES, sl), 0)
    pw = [(a_re, a_im)]
    for _ in range(SUBLANES - 1):
        pw.append(_cmul(pw[-1][0], pw[-1][1], a_re, a_im))
    car_re = jnp.concatenate([p[0] for p in pw], axis=0)
    car_im = jnp.concatenate([p[1] for p in pw], axis=0)
    levels = []
    s = 1
    while s < SUBLANES:
        levels.append((s, jnp.where(rid >= s, pw[s - 1][0], 0.0), jnp.where(rid >= s, pw[s - 1][1], 0.0)))
        s *= 2

    def block(b, carry):
        c_re, c_im = carry
        r0 = pl.multiple_of(b * SUBLANES, SUBLANES)
        g = g_ref[pl.ds(r0, SUBLANES), :]
        h_re, h_im = g[:, :sl], g[:, sl:]
        for s, l_re, l_im in levels:
            s_re, s_im = pltpu.roll(h_re, s, 0), pltpu.roll(h_im, s, 0)
            d_re, d_im = _cmul(l_re, l_im, s_re, s_im)
            h_re, h_im = h_re + d_re, h_im + d_im
        d_re, d_im = _cmul(car_re, car_im, c_re, c_im)
        h_re, h_im = h_re + d_re, h_im + d_im
        p_re = jnp.where(rid == 0, c_re, pltpu.roll(h_re, 1, 0))
        p_im = jnp.where(rid == 0, c_im, pltpu.roll(h_im, 1, 0))
        hp_ref[pl.ds(r0, SUBLANES), :] = jnp.concatenate([p_re, p_im], axis=1)
        return h_re[SUBLANES - 1:, :], h_im[SUBLANES - 1:, :]

    zero = jnp.zeros((1, sl), F32)
    e_re, e_im = lax.fori_loop(0, n_prompt_chunks // SUBLANES, block, (zero, zero))

    h0 = h0_ref[0]
    hp_ref[n_prompt_chunks:, :] = h0
    gs = g_ref[n_prompt_chunks:, :]
    s_re, s_im = _cmul(a_re, a_im, h0[:, :sl], h0[:, sl:])
    hf_ref[0, 0:ns, :] = jnp.concatenate([s_re + gs[:, :sl], s_im + gs[:, sl:]], axis=1)
    hf_ref[0, ns:, :] = jnp.broadcast_to(jnp.concatenate([e_re, e_im], axis=1), (hf_ref.shape[1] - ns, 2 * sl))

    hp = hp_ref[...].astype(BF16)
    dsk = d_ref[...]
    zblk = jnp.zeros((LANES, LANES), BF16)
    for m in range(t // 2):
        j0, j1 = 2 * m, 2 * m + 1
        k0 = jnp.concatenate(krev[t - 1 - j0:] + [zblk], axis=0)
        k1 = jnp.concatenate(krev[t - 1 - j1:], axis=0)
        kp = jnp.concatenate([k0, k1], axis=1)
        y = jnp.dot(un[:, :(j1 + 1) * LANES], kp, preferred_element_type=F32)
        y = y + lax.dot_general(hp, cexp_ref[j0 * LANES:(j1 + 1) * LANES, :], (((1,), (1,)), ((), ())),
                                preferred_element_type=F32)
        ys_ref[pl.ds(j0, nch, stride=t), :] = _gelu_tanh(y[:, :LANES] + dsk * us[j0])
        ys_ref[pl.ds(j1, nch, stride=t), :] = _gelu_tanh(y[:, LANES:] + dsk * us[j1])
    y_ref[...] = ys_ref[...].astype(y_ref.dtype)


def _ssm(xn, tables, h0, d_skip, n_prompt_chunks):
    q, cat, ca_rows, a_t = tables
    m, d = xn.shape
    tiles = d // LANES
    t = SSM_CHUNK
    nch = m // t
    ns = nch - n_prompt_chunks
    assert n_prompt_chunks % SUBLANES == 0 and ns % SUBLANES == 0
    hf_rows = 2 * ns
    tl = t * LANES
    s2 = 2 * STATE_LANES
    p2 = q.shape[-1]
    return pl.pallas_call(
        functools.partial(_ssm_kernel, n_prompt_chunks=n_prompt_chunks),
        grid=(tiles,),
        in_specs=[pl.BlockSpec((m, LANES), lambda i: (0, i)),
                  pl.BlockSpec((1, tl, p2), lambda i: (i, 0, 0)),
                  pl.BlockSpec((1, tl, p2), lambda i: (i, 0, 0)),
                  pl.BlockSpec((LANES, p2), lambda i: (i, 0)),
                  pl.BlockSpec((1, 1, s2), lambda i: (i, 0, 0)),
                  pl.BlockSpec((1, ns, s2), lambda i: (i, 0, 0)),
                  pl.BlockSpec((1, LANES), lambda i: (0, i))],
        out_specs=[pl.BlockSpec((m, LANES), lambda i: (0, i)),
                   pl.BlockSpec((1, hf_rows, s2), lambda i: (i, 0, 0))],
        out_shape=[jax.ShapeDtypeStruct((m, d), BF16),
                   jax.ShapeDtypeStruct((tiles, hf_rows, s2), F32)],
        scratch_shapes=[pltpu.VMEM((m, LANES), F32),
                        pltpu.VMEM((m, LANES), F32),
                        pltpu.VMEM((nch, s2), F32),
                        pltpu.VMEM((nch, s2), F32),
                        pltpu.VMEM((tl, s2), BF16),
                        pltpu.VMEM((tl, s2), BF16)],
        compiler_params=_cparams(("parallel",)),
        name="ssm",
    )(xn, q, cat, ca_rows, a_t, h0, d_skip.reshape(1, d))


def _state_to_tiles(h_re, h_im):
    b, g, p = h_re.shape
    tiles = g // GROUPS_PER_TILE
    re = h_re.reshape(b, tiles, STATE_LANES).transpose(1, 0, 2)
    im = h_im.reshape(b, tiles, STATE_LANES).transpose(1, 0, 2)
    return jnp.concatenate([re, im], axis=-1)


def _tiles_to_state(h):
    tiles, b, _ = h.shape
    re = h[..., :STATE_LANES].transpose(1, 0, 2).reshape(b, tiles * GROUPS_PER_TILE, SSM_STATE)
    im = h[..., STATE_LANES:].transpose(1, 0, 2).reshape(b, tiles * GROUPS_PER_TILE, SSM_STATE)
    return re, im


ROW_TILE = 1664
NORM_TILE = 416
POOL_TILE = 256


def kernel(x_prompt, x_sample, cache_pool, state_ssm_re, state_ssm_im, norm_mix, w_pool, pool_scale, ssm_a_re, ssm_a_im, ssm_log_dt, ssm_b_re, ssm_b_im, ssm_c_re, ssm_c_im, ssm_d, w_glu_a, w_glu_b, norm_ffn, w_up, w_down, norm_final):
    bp, lp, d = x_prompt.shape
    bs, ls, _ = x_sample.shape
    assert bp == 1 and ls == SSM_CHUNK and ls == HIST_PAD and lp % SSM_CHUNK == 0
    ms = bs * ls
    m = lp + ms

    w_pool_bf = w_pool[0].astype(BF16)
    x, xn, hist_p = _pool_prompt(x_prompt[0], norm_mix[0], w_pool_bf, pool_scale[0], norm_ffn[0], POOL_TILE, m)
    hist_s = jnp.pad(cache_pool[0], ((0, 0), (1, 0), (0, 0))).reshape(ms, d)
    x, xn, xn_s = _pool_sample(x_sample.reshape(ms, d), hist_s, norm_mix[0], w_pool_bf, pool_scale[0],
                               norm_ffn[0], ls, x, xn)
    pool_rows_p = hist_p[1:].reshape(1, 1, POOL_HIST, d)
    pool_rows_s = xn_s.reshape(bs, ls, d)[:, 1:].reshape(1, bs, POOL_HIST, d)

    h = _mlp_up(xn, w_up[0], ROW_TILE, 512)
    x = _mlp_down(h, w_down[0], x, ROW_TILE, 1024, 1024)

    xn = _rmsnorm(x, norm_mix[1], BF16, NORM_TILE)
    tables = _ssm_prep(ssm_a_re[0], ssm_a_im[0], ssm_log_dt[0], ssm_b_re[0], ssm_b_im[0],
                       ssm_c_re[0], ssm_c_im[0])
    h0 = _state_to_tiles(state_ssm_re[0], state_ssm_im[0])
    y, hf = _ssm(xn, tables, h0, ssm_d[0], lp // SSM_CHUNK)
    x = _glu(y, w_glu_a[0], w_glu_b[0], x, ROW_TILE, 256)
    re_s, im_s = _tiles_to_state(hf[:, :bs])
    re_p, im_p = _tiles_to_state(hf[:, bs:bs + 1])

    xn = _rmsnorm(x, norm_ffn[1], BF16, NORM_TILE)
    h = _mlp_up(xn, w_up[1], ROW_TILE, 512)
    x = _mlp_down(h, w_down[1], x, ROW_TILE, 1024, 1024)

    y_p = _rmsnorm(x, norm_final, F32, 512, 0, lp).reshape(1, lp, d)
    y_s = _rmsnorm(x, norm_final, F32, ms, lp, ms).reshape(bs, ls, d)
    return (y_p, y_s, pool_rows_p, pool_rows_s, re_p[None], im_p[None], re_s[None], im_s[None])
```

```python
import functools
import math

import jax
import jax.numpy as jnp
from jax import lax
from jax.experimental import pallas as pl
from jax.experimental.pallas import tpu as pltpu

F32 = jnp.float32
BF16 = jnp.bfloat16

EPS = 1e-6
PAST_LEN = 1024
POOL_WINDOWS = (2, 4, 8, 16)
POOL_HIST = max(POOL_WINDOWS) - 1
HIST_PAD = POOL_HIST + 1
SSM_GROUP = 16
SSM_STATE = 64
SSM_CHUNK = 16
LANES = 128
SUBLANES = 8
GROUPS_PER_TILE = LANES // SSM_GROUP
STATE_LANES = GROUPS_PER_TILE * SSM_STATE
VMEM_LIMIT = 56 * 1024 * 1024


def _cparams(sem):
    return pltpu.CompilerParams(dimension_semantics=sem, vmem_limit_bytes=VMEM_LIMIT)


def _rms(x, g):
    return x * lax.rsqrt(jnp.mean(x * x, axis=-1, keepdims=True) + EPS) * g


def _rmsnorm_kernel(x_ref, g_ref, o_ref):
    o_ref[...] = _rms(x_ref[...], g_ref[...]).astype(o_ref.dtype)


def _rmsnorm(x, g, out_dtype, tm):
    m, d = x.shape
    return pl.pallas_call(
        _rmsnorm_kernel,
        grid=(m // tm,),
        in_specs=[pl.BlockSpec((tm, d), lambda i: (i, 0)),
                  pl.BlockSpec((1, d), lambda i: (0, 0))],
        out_specs=pl.BlockSpec((tm, d), lambda i: (i, 0)),
        out_shape=jax.ShapeDtypeStruct((m, d), out_dtype),
        compiler_params=_cparams(("parallel",)),
        name="rmsnorm",
    )(x, g.reshape(1, d))


def _add_norm_kernel(x_ref, dx_ref, g_ref, xo_ref, xn_ref):
    x = x_ref[...] + dx_ref[...].astype(F32)
    xo_ref[...] = x
    xn_ref[...] = _rms(x, g_ref[...]).astype(xn_ref.dtype)


def _add_norm(x, dx, g, tm):
    m, d = x.shape
    row = pl.BlockSpec((tm, d), lambda i: (i, 0))
    return pl.pallas_call(
        _add_norm_kernel,
        grid=(m // tm,),
        in_specs=[row, row, pl.BlockSpec((1, d), lambda i: (0, 0))],
        out_specs=[row, row],
        out_shape=[jax.ShapeDtypeStruct((m, d), F32), jax.ShapeDtypeStruct((m, d), BF16)],
        compiler_params=_cparams(("parallel",)),
        name="add_norm",
    )(x, dx, g.reshape(1, d))


def _add_norm_out_kernel(x_ref, dx_ref, g_ref, o_ref):
    o_ref[...] = _rms(x_ref[...] + dx_ref[...].astype(F32), g_ref[...])


def _add_norm_out(x, dx, g, tm, row0, rows):
    d = x.shape[1]
    assert rows % tm == 0 and row0 % tm == 0
    off = row0 // tm
    src = pl.BlockSpec((tm, d), lambda i: (i + off, 0))
    return pl.pallas_call(
        _add_norm_out_kernel,
        grid=(rows // tm,),
        in_specs=[src, src, pl.BlockSpec((1, d), lambda i: (0, 0))],
        out_specs=pl.BlockSpec((tm, d), lambda i: (i, 0)),
        out_shape=jax.ShapeDtypeStruct((rows, d), F32),
        compiler_params=_cparams(("parallel",)),
        name="add_norm_out",
    )(x, dx, g.reshape(1, d))


def _pool_prompt_kernel(x_ref, gm_ref, w_ref, sc_ref, gf_ref, x1_ref, xnf_ref, hist_ref, ext_ref, *, tm):
    i = pl.program_id(0)
    d = x_ref.shape[1]
    pg = d // len(POOL_WINDOWS)

    @pl.when(i == 0)
    def _():
        ext_ref[0:HIST_PAD, :] = jnp.zeros((HIST_PAD, d), F32)

    x = x_ref[...]
    ext_ref[HIST_PAD:, :] = _rms(x, gm_ref[...])

    pos = i * tm + lax.broadcasted_iota(jnp.int32, (tm, 1), 0)
    ss = jnp.zeros((tm, 1), F32)
    for gi, w in enumerate(POOL_WINDOWS):
        sl = slice(gi * pg, (gi + 1) * pg)
        a = ext_ref[:, sl]
        s = a
        span = 1
        while span < w:
            s = s + pltpu.roll(s, span, 0)
            span *= 2
        inv_cnt = 1.0 / jnp.minimum(w, pos + 1).astype(F32)
        pooled = s[HIST_PAD:, :] * inv_cnt - a[HIST_PAD:, :]
        mixed = jnp.dot(pooled.astype(BF16), w_ref[gi], preferred_element_type=F32)
        x1 = x[:, sl] + mixed * sc_ref[:, sl]
        x1_ref[:, sl] = x1
        ss = ss + jnp.sum(x1 * x1, axis=-1, keepdims=True)

    inv = lax.rsqrt(ss / d + EPS)
    xnf_ref[...] = (x1_ref[...] * inv * gf_ref[...]).astype(xnf_ref.dtype)
    hist_ref[...] = ext_ref[tm:, :]
    ext_ref[0:HIST_PAD, :] = ext_ref[tm:, :]


def _pool_prompt(x, g_mix, w_pool_bf, scale, g_ffn, tm, m_total):
    l, d = x.shape
    ng, pg, _ = w_pool_bf.shape
    return pl.pallas_call(
        functools.partial(_pool_prompt_kernel, tm=tm),
        grid=(l // tm,),
        in_specs=[pl.BlockSpec((tm, d), lambda i: (i, 0)),
                  pl.BlockSpec((1, d), lambda i: (0, 0)),
                  pl.BlockSpec((ng, pg, pg), lambda i: (0, 0, 0)),
                  pl.BlockSpec((1, d), lambda i: (0, 0)),
                  pl.BlockSpec((1, d), lambda i: (0, 0))],
        out_specs=[pl.BlockSpec((tm, d), lambda i: (i, 0)),
                   pl.BlockSpec((tm, d), lambda i: (i, 0)),
                   pl.BlockSpec((HIST_PAD, d), lambda i: (0, 0))],
        out_shape=[jax.ShapeDtypeStruct((m_total, d), F32),
                   jax.ShapeDtypeStruct((m_total, d), BF16),
                   jax.ShapeDtypeStruct((HIST_PAD, d), F32)],
        scratch_shapes=[pltpu.VMEM((HIST_PAD + tm, d), F32)],
        compiler_params=_cparams(("arbitrary",)),
        name="pool_prompt",
    )(x, g_mix.reshape(1, d), w_pool_bf, scale.reshape(1, d), g_ffn.reshape(1, d))


def _pool_sample_kernel(x_ref, h_ref, gm_ref, w_ref, sc_ref, gf_ref, x1_in, xnf_in, x1_ref, xnf_ref, xn_ref, *, seq):
    del x1_in, xnf_in
    m, d = x_ref.shape
    pg = d // len(POOL_WINDOWS)
    x = x_ref[...]
    xn_ref[...] = _rms(x, gm_ref[...])
    t_in_seq = lax.broadcasted_iota(jnp.int32, (m, 1), 0) % seq
    ss = jnp.zeros((m, 1), F32)
    for gi, w in enumerate(POOL_WINDOWS):
        sl = slice(gi * pg, (gi + 1) * pg)
        a = xn_ref[:, sl]
        h = h_ref[:, sl]
        s = a
        for lag in range(1, w):
            cur = pltpu.roll(a, lag, 0)
            old = pltpu.roll(h, (m - HIST_PAD + lag) % m, 0)
            s = s + jnp.where(t_in_seq >= lag, cur, old)
        pooled = s / float(w) - a
        mixed = jnp.dot(pooled.astype(BF16), w_ref[gi], preferred_element_type=F32)
        x1 = x[:, sl] + mixed * sc_ref[:, sl]
        x1_ref[:, sl] = x1
        ss = ss + jnp.sum(x1 * x1, axis=-1, keepdims=True)
    inv = lax.rsqrt(ss / d + EPS)
    xnf_ref[...] = (x1_ref[...] * inv * gf_ref[...]).astype(xnf_ref.dtype)


def _pool_sample(x, hist, g_mix, w_pool_bf, scale, g_ffn, seq, x1_all, xnf_all):
    m, d = x.shape
    m_total = x1_all.shape[0]
    assert (m_total - m) % m == 0
    last = (m_total - m) // m
    ng, pg, _ = w_pool_bf.shape
    full = lambda shape: pl.BlockSpec(shape, lambda i: (0,) * len(shape))
    anyspec = pl.BlockSpec(memory_space=pl.ANY)
    return pl.pallas_call(
        functools.partial(_pool_sample_kernel, seq=seq),
        grid=(1,),
        in_specs=[full((m, d)), full((m, d)), full((1, d)), full((ng, pg, pg)), full((1, d)), full((1, d)),
                  anyspec, anyspec],
        out_specs=[pl.BlockSpec((m, d), lambda i: (last, 0)),
                   pl.BlockSpec((m, d), lambda i: (last, 0)),
                   full((m, d))],
        out_shape=[jax.ShapeDtypeStruct((m_total, d), F32),
                   jax.ShapeDtypeStruct((m_total, d), BF16),
                   jax.ShapeDtypeStruct((m, d), F32)],
        input_output_aliases={6: 0, 7: 1},
        compiler_params=_cparams(("arbitrary",)),
        name="pool_sample",
    )(x, hist, g_mix.reshape(1, d), w_pool_bf, scale.reshape(1, d), g_ffn.reshape(1, d), x1_all, xnf_all)


def _up_kernel(x_ref, w_ref, o_ref):
    acc = jnp.dot(x_ref[...], w_ref[...].astype(BF16), preferred_element_type=F32)
    h = jnp.maximum(acc, 0.0)
    o_ref[...] = (h * h).astype(o_ref.dtype)


def _mlp_up(xn, w, layer, tm, tn):
    m, k = xn.shape
    n = w.shape[2]
    return pl.pallas_call(
        _up_kernel,
        grid=(m // tm, n // tn),
        in_specs=[pl.BlockSpec((tm, k), lambda i, j: (i, 0), pipeline_mode=pl.Buffered(1)),
                  pl.BlockSpec((None, k, tn), lambda i, j: (layer, 0, j))],
        out_specs=pl.BlockSpec((tm, tn), lambda i, j: (i, j)),
        out_shape=jax.ShapeDtypeStruct((m, n), BF16),
        compiler_params=_cparams(("parallel", "arbitrary")),
        name="mlp_up",
    )(xn, w)


def _down_kernel(h_ref, w_ref, o_ref, acc_ref):
    kk = pl.program_id(2)

    @pl.when(kk == 0)
    def _():
        acc_ref[...] = jnp.zeros_like(acc_ref)

    acc_ref[...] += jnp.dot(h_ref[...], w_ref[...].astype(BF16), preferred_element_type=F32)

    @pl.when(kk == pl.num_programs(2) - 1)
    def _():
        o_ref[...] = acc_ref[...].astype(o_ref.dtype)


def _mlp_down(h, w, layer, tm, tn, tk):
    m, k = h.shape
    n = w.shape[2]
    return pl.pallas_call(
        _down_kernel,
        grid=(m // tm, n // tn, k // tk),
        in_specs=[pl.BlockSpec((tm, tk), lambda i, j, kk: (i, kk)),
                  pl.BlockSpec((None, tk, tn), lambda i, j, kk: (layer, kk, j))],
        out_specs=pl.BlockSpec((tm, tn), lambda i, j, kk: (i, j)),
        out_shape=jax.ShapeDtypeStruct((m, n), BF16),
        scratch_shapes=[pltpu.VMEM((tm, tn), F32)],
        compiler_params=_cparams(("parallel", "parallel", "arbitrary")),
        name="mlp_down",
    )(h, w)


def _glu_kernel(y_ref, wa_ref, wb_ref, r_ref, o_ref):
    y = y_ref[...]
    a = jnp.dot(y, wa_ref[...].astype(BF16), preferred_element_type=F32)
    b = jnp.dot(y, wb_ref[...].astype(BF16), preferred_element_type=F32)
    o_ref[...] = r_ref[...] + a * jax.nn.sigmoid(b)


def _glu(y, wa, wb, layer, resid, tm, tn):
    m, k = y.shape
    n = wa.shape[2]
    wspec = pl.BlockSpec((None, k, tn), lambda i, j: (layer, 0, j))
    return pl.pallas_call(
        _glu_kernel,
        grid=(m // tm, n // tn),
        in_specs=[pl.BlockSpec((tm, k), lambda i, j: (i, 0), pipeline_mode=pl.Buffered(1)),
                  wspec, wspec,
                  pl.BlockSpec((tm, tn), lambda i, j: (i, j))],
        out_specs=pl.BlockSpec((tm, tn), lambda i, j: (i, j)),
        out_shape=jax.ShapeDtypeStruct((m, n), F32),
        compiler_params=_cparams(("parallel", "arbitrary")),
        name="glu",
    )(y, wa, wb, resid)


def _cmul(a_re, a_im, b_re, b_im):
    return a_re * b_re - a_im * b_im, a_re * b_im + a_im * b_re


def _ssm_prep_kernel(are_ref, aim_ref, ldt_ref, brp_ref, bpr_ref, ca_ref, cb_ref, q_ref, cat_ref, pt_ref):
    t = SSM_CHUNK
    rows = are_ref.shape[0]
    l_re, l_im = are_ref[...], aim_ref[...]
    dt = jnp.exp(ldt_ref[...])
    mag = jnp.exp(l_re * dt)
    a_re, a_im = mag * jnp.cos(l_im * dt), mag * jnp.sin(l_im * dt)
    den = l_re * l_re + l_im * l_im
    n_re, n_im = a_re - 1.0, a_im
    k_re = (n_re * l_re + n_im * l_im) / den
    k_im = (n_im * l_re - n_re * l_im) / den
    b_rp, b_pr = brp_ref[...], bpr_ref[...]
    bb_rp = k_re * b_rp + k_im * b_pr
    bb_pr = k_re * b_pr - k_im * b_rp
    c_a, c_b = ca_ref[...], cb_ref[...]
    p_re, p_im = jnp.ones_like(a_re), jnp.zeros_like(a_re)
    for n in range(t):
        j = t - 1 - n
        q_ref[0, j * rows:(j + 1) * rows, :] = (p_re * bb_rp + p_im * bb_pr).astype(q_ref.dtype)
        p_re, p_im = _cmul(p_re, p_im, a_re, a_im)
        cat_ref[0, :, n * rows:(n + 1) * rows] = (p_re * c_a + p_im * c_b).T.astype(cat_ref.dtype)
    half = lax.broadcasted_iota(jnp.int32, p_re.shape, 1) < (p_re.shape[1] // 2)
    pt_ref[...] = jnp.where(half, p_re, p_im)


def _ssm_prep(a_re, a_im, log_dt, b_re, b_im, c_re, c_im):
    g, p = a_re.shape
    c = b_re.shape[-1]
    t = SSM_CHUNK
    tiles = g // GROUPS_PER_TILE
    rows = g * c
    assert 2 * p == LANES
    dup = lambda v: jnp.concatenate([v, v], axis=-1)
    per_row = lambda v: dup(jnp.repeat(v, c, axis=0))
    bt_re = b_re.transpose(0, 2, 1).reshape(rows, p)
    bt_im = b_im.transpose(0, 2, 1).reshape(rows, p)
    cr, ci = c_re.reshape(rows, p), c_im.reshape(rows, p)
    args = (per_row(a_re), per_row(a_im), per_row(jnp.broadcast_to(log_dt[:, None], (g, p))),
            jnp.concatenate([bt_re, bt_im], axis=-1), jnp.concatenate([-bt_im, bt_re], axis=-1),
            jnp.concatenate([cr, -ci], axis=-1), jnp.concatenate([-ci, -cr], axis=-1))
    blk = pl.BlockSpec((LANES, LANES), lambda i: (i, 0))
    q, cat, pt = pl.pallas_call(
        _ssm_prep_kernel,
        grid=(tiles,),
        in_specs=[blk] * 7,
        out_specs=[pl.BlockSpec((1, t * LANES, LANES), lambda i: (i, 0, 0)),
                   pl.BlockSpec((1, LANES, t * LANES), lambda i: (i, 0, 0)),
                   blk],
        out_shape=[jax.ShapeDtypeStruct((tiles, t * LANES, LANES), BF16),
                   jax.ShapeDtypeStruct((tiles, LANES, t * LANES), BF16),
                   jax.ShapeDtypeStruct((rows, LANES), F32)],
        compiler_params=_cparams(("parallel",)),
        name="ssm_prep",
    )(*args)
    pt = pt.reshape(tiles, GROUPS_PER_TILE, c, 2, p)[:, :, 0]
    a_t = pt.transpose(0, 2, 1, 3).reshape(tiles, 1, 2 * STATE_LANES)
    ca_rows = args[5].astype(BF16)
    return q, cat, ca_rows, a_t


def _gelu_tanh(x):
    return 0.5 * x * (1.0 + jnp.tanh(math.sqrt(2.0 / math.pi) * (x + 0.044715 * x * x * x)))


SSM_POS_PER_DOT = 4


def _ssm_kernel(x_ref, q_ref, cat_ref, ca_ref, at_ref, h0_ref, d_ref, y_ref, hf_ref,
                xs_ref, ys_ref, g_ref, hp_ref, bexp_ref, cexp_ref, *, n_prompt_chunks):
    t = SSM_CHUNK
    nq = SSM_POS_PER_DOT
    qw = nq * LANES
    nch = x_ref.shape[0] // t
    ns = nch - n_prompt_chunks
    sl = STATE_LANES
    ii = lambda shape, dim: lax.broadcasted_iota(jnp.int32, shape, dim)

    r2, c2 = ii((LANES, 2 * sl), 0), ii((LANES, 2 * sl), 1)
    rep = ((r2 // SSM_STATE == c2 // sl) & (r2 % SSM_STATE == c2 % SSM_STATE)).astype(BF16)
    r2, c2 = ii((qw, 2 * sl), 0), ii((qw, 2 * sl), 1)
    own = (r2 % LANES) // SSM_GROUP == (c2 // SSM_STATE) % GROUPS_PER_TILE
    for b in range(t // nq):
        rows = slice(b * qw, (b + 1) * qw)
        e = jnp.dot(q_ref[0, rows, :], rep, preferred_element_type=F32)
        bexp_ref[rows, :] = jnp.where(own, e, 0.0).astype(BF16)
    r2, c2 = ii((2 * sl, LANES), 0), ii((2 * sl, LANES), 1)
    rep_t = ((c2 // SSM_STATE == r2 // sl) & (c2 % SSM_STATE == r2 % SSM_STATE)).astype(BF16)
    r2, c2 = ii((2 * sl, qw), 0), ii((2 * sl, qw), 1)
    own_t = (c2 % LANES) // SSM_GROUP == (r2 // SSM_STATE) % GROUPS_PER_TILE
    for b in range(t // nq):
        cols = slice(b * qw, (b + 1) * qw)
        e = jnp.dot(rep_t, cat_ref[0, :, cols], preferred_element_type=F32)
        cexp_ref[:, cols] = jnp.where(own_t, e, 0.0).astype(BF16)
    kr = lax.dot_general(q_ref[0], ca_ref[...], (((1,), (1,)), ((), ())), preferred_element_type=F32)
    same = (ii((LANES, LANES), 0) // SSM_GROUP) == (ii((LANES, LANES), 1) // SSM_GROUP)
    krev = [jnp.where(same, kr[j * LANES:(j + 1) * LANES, :], 0.0).astype(BF16) for j in range(t)]

    xs_ref[...] = x_ref[...].astype(F32)
    us = [xs_ref[pl.ds(j, nch, stride=t), :] for j in range(t)]
    un = jnp.concatenate([u.astype(BF16) for u in us], axis=1)
    g_ref[...] = jnp.dot(un, bexp_ref[...], preferred_element_type=F32)

    at = at_ref[0]
    a_re, a_im = at[:, :sl], at[:, sl:]
    rid = ii((SUBLANES, sl), 0)
    pw = [(a_re, a_im)]
    for _ in range(SUBLANES - 1):
        pw.append(_cmul(pw[-1][0], pw[-1][1], a_re, a_im))
    car_re = jnp.concatenate([p[0] for p in pw], axis=0)
    car_im = jnp.concatenate([p[1] for p in pw], axis=0)
    levels = []
    s = 1
    while s < SUBLANES:
        levels.append((s, jnp.where(rid >= s, pw[s - 1][0], 0.0), jnp.where(rid >= s, pw[s - 1][1], 0.0)))
        s *= 2

    def block(b, carry):
        c_re, c_im = carry
        r0 = pl.multiple_of(b * SUBLANES, SUBLANES)
        g = g_ref[pl.ds(r0, SUBLANES), :]
        h_re, h_im = g[:, :sl], g[:, sl:]
        for s, l_re, l_im in levels:
            s_re, s_im = pltpu.roll(h_re, s, 0), pltpu.roll(h_im, s, 0)
            d_re, d_im = _cmul(l_re, l_im, s_re, s_im)
            h_re, h_im = h_re + d_re, h_im + d_im
        d_re, d_im = _cmul(car_re, car_im, c_re, c_im)
        h_re, h_im = h_re + d_re, h_im + d_im
        p_re = jnp.where(rid == 0, c_re, pltpu.roll(h_re, 1, 0))
        p_im = jnp.where(rid == 0, c_im, pltpu.roll(h_im, 1, 0))
        hp_ref[pl.ds(r0, SUBLANES), :] = jnp.concatenate([p_re, p_im], axis=1)
        return h_re[SUBLANES - 1:, :], h_im[SUBLANES - 1:, :]

    zero = jnp.zeros((1, sl), F32)
    e_re, e_im = lax.fori_loop(0, n_prompt_chunks // SUBLANES, block, (zero, zero))

    h0 = h0_ref[0]
    hp_ref[n_prompt_chunks:, :] = h0
    gs = g_ref[n_prompt_chunks:, :]
    s_re, s_im = _cmul(a_re, a_im, h0[:, :sl], h0[:, sl:])
    hf_ref[0, 0:ns, :] = jnp.concatenate([s_re + gs[:, :sl], s_im + gs[:, sl:]], axis=1)
    hf_ref[0, ns:, :] = jnp.broadcast_to(jnp.concatenate([e_re, e_im], axis=1), (hf_ref.shape[1] - ns, 2 * sl))

    hp = hp_ref[...].astype(BF16)
    dsk = d_ref[...]
    zblk = jnp.zeros((LANES, LANES), BF16)
    for b in range(t // nq):
        j_hi = (b + 1) * nq
        cols = []
        for j in range(b * nq, j_hi):
            cols.append(jnp.concatenate(krev[t - 1 - j:] + [zblk] * (j_hi - 1 - j), axis=0))
        kq = jnp.concatenate(cols, axis=1)
        y = jnp.dot(un[:, :j_hi * LANES], kq, preferred_element_type=F32)
        y = y + jnp.dot(hp, cexp_ref[:, b * qw:(b + 1) * qw], preferred_element_type=F32)
        for jj in range(nq):
            j = b * nq + jj
            ys_ref[pl.ds(j, nch, stride=t), :] = _gelu_tanh(y[:, jj * LANES:(jj + 1) * LANES] + dsk * us[j])
    y_ref[...] = ys_ref[...].astype(y_ref.dtype)


def _ssm(xn, tables, h0, d_skip, n_prompt_chunks):
    q, cat, ca_rows, a_t = tables
    m, d = xn.shape
    tiles = d // LANES
    t = SSM_CHUNK
    nch = m // t
    ns = nch - n_prompt_chunks
    assert n_prompt_chunks % SUBLANES == 0 and ns % SUBLANES == 0 and t % SSM_POS_PER_DOT == 0
    hf_rows = 2 * ns
    tl = t * LANES
    s2 = 2 * STATE_LANES
    return pl.pallas_call(
        functools.partial(_ssm_kernel, n_prompt_chunks=n_prompt_chunks),
        grid=(tiles,),
        in_specs=[pl.BlockSpec((m, LANES), lambda i: (0, i)),
                  pl.BlockSpec((1, tl, LANES), lambda i: (i, 0, 0)),
                  pl.BlockSpec((1, LANES, tl), lambda i: (i, 0, 0)),
                  pl.BlockSpec((LANES, LANES), lambda i: (i, 0)),
                  pl.BlockSpec((1, 1, s2), lambda i: (i, 0, 0)),
                  pl.BlockSpec((1, ns, s2), lambda i: (i, 0, 0)),
                  pl.BlockSpec((1, LANES), lambda i: (0, i))],
        out_specs=[pl.BlockSpec((m, LANES), lambda i: (0, i)),
                   pl.BlockSpec((1, hf_rows, s2), lambda i: (i, 0, 0))],
        out_shape=[jax.ShapeDtypeStruct((m, d), BF16),
                   jax.ShapeDtypeStruct((tiles, hf_rows, s2), F32)],
        scratch_shapes=[pltpu.VMEM((m, LANES), F32),
                        pltpu.VMEM((m, LANES), F32),
                        pltpu.VMEM((nch, s2), F32),
                        pltpu.VMEM((nch, s2), F32),
                        pltpu.VMEM((tl, s2), BF16),
                        pltpu.VMEM((s2, tl), BF16)],
        compiler_params=_cparams(("parallel",)),
        name="ssm",
    )(xn, q, cat, ca_rows, a_t, h0, d_skip.reshape(1, d))


def _state_to_tiles(h_re, h_im):
    b, g, p = h_re.shape
    tiles = g // GROUPS_PER_TILE
    re = h_re.reshape(b, tiles, STATE_LANES).transpose(1, 0, 2)
    im = h_im.reshape(b, tiles, STATE_LANES).transpose(1, 0, 2)
    return jnp.concatenate([re, im], axis=-1)


def _tiles_to_state(h):
    tiles, b, _ = h.shape
    re = h[..., :STATE_LANES].transpose(1, 0, 2).reshape(b, tiles * GROUPS_PER_TILE, SSM_STATE)
    im = h[..., STATE_LANES:].transpose(1, 0, 2).reshape(b, tiles * GROUPS_PER_TILE, SSM_STATE)
    return re, im


ROW_TILE = 1664
NORM_TILE = 416
POOL_TILE = 256
UP_COLS = 512
DOWN_COLS, DOWN_DEPTH = 1024, 2048
GLU_COLS = 256


def kernel(x_prompt, x_sample, cache_pool, state_ssm_re, state_ssm_im, norm_mix, w_pool, pool_scale, ssm_a_re, ssm_a_im, ssm_log_dt, ssm_b_re, ssm_b_im, ssm_c_re, ssm_c_im, ssm_d, w_glu_a, w_glu_b, norm_ffn, w_up, w_down, norm_final):
    bp, lp, d = x_prompt.shape
    bs, ls, _ = x_sample.shape
    assert bp == 1 and ls == SSM_CHUNK and ls == HIST_PAD and lp % SSM_CHUNK == 0
    ms = bs * ls
    m = lp + ms

    w_pool_bf = w_pool[0].astype(BF16)
    x, xn, hist_p = _pool_prompt(x_prompt[0], norm_mix[0], w_pool_bf, pool_scale[0], norm_ffn[0], POOL_TILE, m)
    hist_s = jnp.pad(cache_pool[0], ((0, 0), (1, 0), (0, 0))).reshape(ms, d)
    x, xn, xn_s = _pool_sample(x_sample.reshape(ms, d), hist_s, norm_mix[0], w_pool_bf, pool_scale[0],
                               norm_ffn[0], ls, x, xn)
    pool_rows_p = hist_p[1:].reshape(1, 1, POOL_HIST, d)
    pool_rows_s = xn_s.reshape(bs, ls, d)[:, 1:].reshape(1, bs, POOL_HIST, d)

    h = _mlp_up(xn, w_up, 0, ROW_TILE, UP_COLS)
    dx = _mlp_down(h, w_down, 0, ROW_TILE, DOWN_COLS, DOWN_DEPTH)
    x, xn = _add_norm(x, dx, norm_mix[1], NORM_TILE)

    tables = _ssm_prep(ssm_a_re[0], ssm_a_im[0], ssm_log_dt[0], ssm_b_re[0], ssm_b_im[0],
                       ssm_c_re[0], ssm_c_im[0])
    h0 = _state_to_tiles(state_ssm_re[0], state_ssm_im[0])
    y, hf = _ssm(xn, tables, h0, ssm_d[0], lp // SSM_CHUNK)
    x = _glu(y, w_glu_a, w_glu_b, 0, x, ROW_TILE, GLU_COLS)
    re_s, im_s = _tiles_to_state(hf[:, :bs])
    re_p, im_p = _tiles_to_state(hf[:, bs:bs + 1])

    xn = _rmsnorm(x, norm_ffn[1], BF16, NORM_TILE)
    h = _mlp_up(xn, w_up, 1, ROW_TILE, UP_COLS)
    dx = _mlp_down(h, w_down, 1, ROW_TILE, DOWN_COLS, DOWN_DEPTH)
    y_p = _add_norm_out(x, dx, norm_final, 512, 0, lp).reshape(1, lp, d)
    y_s = _add_norm_out(x, dx, norm_final, ms, lp, ms).reshape(bs, ls, d)
    return (y_p, y_s, pool_rows_p, pool_rows_s, re_p[None], im_p[None], re_s[None], im_s[None])
```

```python
import functools
import math

import jax
import jax.numpy as jnp
from jax import lax
from jax.experimental import pallas as pl
from jax.experimental.pallas import tpu as pltpu

F32 = jnp.float32
BF16 = jnp.bfloat16

EPS = 1e-6
PAST_LEN = 1024
POOL_WINDOWS = (2, 4, 8, 16)
POOL_HIST = max(POOL_WINDOWS) - 1
HIST_PAD = POOL_HIST + 1
SSM_GROUP = 16
SSM_STATE = 64
SSM_CHUNK = 16
LANES = 128
SUBLANES = 8
GROUPS_PER_TILE = LANES // SSM_GROUP
STATE_LANES = GROUPS_PER_TILE * SSM_STATE
VMEM_LIMIT = 56 * 1024 * 1024


def _cparams(sem):
    return pltpu.CompilerParams(dimension_semantics=sem, vmem_limit_bytes=VMEM_LIMIT)


def _rms(x, g):
    return x * lax.rsqrt(jnp.mean(x * x, axis=-1, keepdims=True) + EPS) * g


def _rmsnorm_kernel(x_ref, g_ref, o_ref):
    o_ref[...] = _rms(x_ref[...], g_ref[...]).astype(o_ref.dtype)


def _rmsnorm(x, g, out_dtype, tm):
    m, d = x.shape
    return pl.pallas_call(
        _rmsnorm_kernel,
        grid=(m // tm,),
        in_specs=[pl.BlockSpec((tm, d), lambda i: (i, 0)),
                  pl.BlockSpec((1, d), lambda i: (0, 0))],
        out_specs=pl.BlockSpec((tm, d), lambda i: (i, 0)),
        out_shape=jax.ShapeDtypeStruct((m, d), out_dtype),
        compiler_params=_cparams(("parallel",)),
        name="rmsnorm",
    )(x, g.reshape(1, d))


def _add_norm_kernel(x_ref, dx_ref, g_ref, o_ref):
    o_ref[...] = _rms(x_ref[...] + dx_ref[...].astype(F32), g_ref[...]).astype(o_ref.dtype)


def _add_norm(x, dx, g, out_dtype, tm, row0=0, rows=None):
    m, d = x.shape
    rows = m - row0 if rows is None else rows
    assert rows % tm == 0 and row0 % tm == 0
    off = row0 // tm
    src = pl.BlockSpec((tm, d), lambda i: (i + off, 0))
    return pl.pallas_call(
        _add_norm_kernel,
        grid=(rows // tm,),
        in_specs=[src, src, pl.BlockSpec((1, d), lambda i: (0, 0))],
        out_specs=pl.BlockSpec((tm, d), lambda i: (i, 0)),
        out_shape=jax.ShapeDtypeStruct((rows, d), out_dtype),
        compiler_params=_cparams(("parallel",)),
        name="add_norm",
    )(x, dx, g.reshape(1, d))


def _pool_finish(x, a, s_scaled, gi, pg, w_ref, sc_ref, x1_ref, rows):
    sl = slice(gi * pg, (gi + 1) * pg)
    pooled = s_scaled - a
    mixed = jnp.dot(pooled.astype(BF16), w_ref[gi], preferred_element_type=F32)
    x1 = x[:, sl] + mixed * sc_ref[:, sl]
    x1_ref[0:rows, sl] = x1
    return jnp.sum(x1 * x1, axis=-1, keepdims=True)


def _pool_kernel(xp_ref, xs_ref, hs_ref, gm_ref, w_ref, sc_ref, gf_ref,
                 x1_ref, xnf_ref, hist_ref, xns_ref, ext_ref, *, tm, n_prompt_tiles, seq):
    i = pl.program_id(0)
    d = xp_ref.shape[1]
    pg = d // len(POOL_WINDOWS)

    @pl.when(i == 0)
    def _():
        ext_ref[0:HIST_PAD, :] = jnp.zeros((HIST_PAD, d), F32)

    @pl.when(i < n_prompt_tiles)
    def _():
        x = xp_ref[...]
        ext_ref[HIST_PAD:, :] = _rms(x, gm_ref[...])
        pos = i * tm + lax.broadcasted_iota(jnp.int32, (tm, 1), 0)
        ss = jnp.zeros((tm, 1), F32)
        for gi, w in enumerate(POOL_WINDOWS):
            a = ext_ref[:, gi * pg:(gi + 1) * pg]
            s = a
            span = 1
            while span < w:
                s = s + pltpu.roll(s, span, 0)
                span *= 2
            inv_cnt = 1.0 / jnp.minimum(w, pos + 1).astype(F32)
            ss = ss + _pool_finish(x, a[HIST_PAD:, :], s[HIST_PAD:, :] * inv_cnt, gi, pg, w_ref, sc_ref, x1_ref, tm)
        inv = lax.rsqrt(ss / d + EPS)
        xnf_ref[...] = (x1_ref[...] * inv * gf_ref[...]).astype(xnf_ref.dtype)
        hist_ref[...] = ext_ref[tm:, :]
        ext_ref[0:HIST_PAD, :] = ext_ref[tm:, :]

    @pl.when(i == n_prompt_tiles)
    def _():
        ms = xs_ref.shape[0]
        x = xs_ref[...]
        xns_ref[...] = _rms(x, gm_ref[...])
        t_in_seq = lax.broadcasted_iota(jnp.int32, (ms, 1), 0) % seq
        ss = jnp.zeros((ms, 1), F32)
        for gi, w in enumerate(POOL_WINDOWS):
            sl = slice(gi * pg, (gi + 1) * pg)
            a = xns_ref[:, sl]
            h = hs_ref[:, sl]
            s = a
            for lag in range(1, w):
                cur = pltpu.roll(a, lag, 0)
                old = pltpu.roll(h, (ms - HIST_PAD + lag) % ms, 0)
                s = s + jnp.where(t_in_seq >= lag, cur, old)
            ss = ss + _pool_finish(x, a, s / float(w), gi, pg, w_ref, sc_ref, x1_ref, ms)
        inv = lax.rsqrt(ss / d + EPS)
        xnf_ref[0:ms, :] = (x1_ref[0:ms, :] * inv * gf_ref[...]).astype(xnf_ref.dtype)


def _pool_layer(x_prompt, x_sample, hist_sample, g_mix, w_pool_bf, scale, g_ffn, tm, seq):
    l, d = x_prompt.shape
    ms = x_sample.shape[0]
    assert l % tm == 0 and ms <= tm and PAST_LEN >= POOL_HIST
    nt = l // tm
    ng, pg, _ = w_pool_bf.shape
    vec = pl.BlockSpec((1, d), lambda i: (0, 0))
    whole = pl.BlockSpec((ms, d), lambda i: (0, 0))
    tile = pl.BlockSpec((tm, d), lambda i: (i, 0))
    return pl.pallas_call(
        functools.partial(_pool_kernel, tm=tm, n_prompt_tiles=nt, seq=seq),
        grid=(nt + 1,),
        in_specs=[pl.BlockSpec((tm, d), lambda i: (jnp.minimum(i, nt - 1), 0)), whole, whole, vec,
                  pl.BlockSpec((ng, pg, pg), lambda i: (0, 0, 0)), vec, vec],
        out_specs=[tile, tile, pl.BlockSpec((HIST_PAD, d), lambda i: (0, 0)), whole],
        out_shape=[jax.ShapeDtypeStruct((l + ms, d), F32),
                   jax.ShapeDtypeStruct((l + ms, d), BF16),
                   jax.ShapeDtypeStruct((HIST_PAD, d), F32),
                   jax.ShapeDtypeStruct((ms, d), F32)],
        scratch_shapes=[pltpu.VMEM((HIST_PAD + tm, d), F32)],
        compiler_params=_cparams(("arbitrary",)),
        name="pool_layer",
    )(x_prompt, x_sample, hist_sample, g_mix.reshape(1, d), w_pool_bf, scale.reshape(1, d), g_ffn.reshape(1, d))


def _up_kernel(x_ref, w_ref, o_ref):
    acc = jnp.dot(x_ref[...], w_ref[...].astype(BF16), preferred_element_type=F32)
    h = jnp.maximum(acc, 0.0)
    o_ref[...] = (h * h).astype(o_ref.dtype)


def _mlp_up(xn, w, layer, tm, tn):
    m, k = xn.shape
    n = w.shape[2]
    return pl.pallas_call(
        _up_kernel,
        grid=(m // tm, n // tn),
        in_specs=[pl.BlockSpec((tm, k), lambda i, j: (i, 0), pipeline_mode=pl.Buffered(1)),
                  pl.BlockSpec((None, k, tn), lambda i, j: (layer, 0, j))],
        out_specs=pl.BlockSpec((tm, tn), lambda i, j: (i, j)),
        out_shape=jax.ShapeDtypeStruct((m, n), BF16),
        compiler_params=_cparams(("parallel", "arbitrary")),
        name="mlp_up",
    )(xn, w)


def _down_kernel(h_ref, w_ref, o_ref, acc_ref):
    kk = pl.program_id(2)

    @pl.when(kk == 0)
    def _():
        acc_ref[...] = jnp.zeros_like(acc_ref)

    acc_ref[...] += jnp.dot(h_ref[...], w_ref[...].astype(BF16), preferred_element_type=F32)

    @pl.when(kk == pl.num_programs(2) - 1)
    def _():
        o_ref[...] = acc_ref[...].astype(o_ref.dtype)


def _mlp_down(h, w, layer, tm, tn, tk):
    m, k = h.shape
    n = w.shape[2]
    return pl.pallas_call(
        _down_kernel,
        grid=(m // tm, n // tn, k // tk),
        in_specs=[pl.BlockSpec((tm, tk), lambda i, j, kk: (i, kk)),
                  pl.BlockSpec((None, tk, tn), lambda i, j, kk: (layer, kk, j))],
        out_specs=pl.BlockSpec((tm, tn), lambda i, j, kk: (i, j)),
        out_shape=jax.ShapeDtypeStruct((m, n), BF16),
        scratch_shapes=[pltpu.VMEM((tm, tn), F32)],
        compiler_params=_cparams(("parallel", "parallel", "arbitrary")),
        name="mlp_down",
    )(h, w)


def _glu_kernel(y_ref, wa_ref, wb_ref, r_ref, dr_ref, o_ref):
    y = y_ref[...]
    a = jnp.dot(y, wa_ref[...].astype(BF16), preferred_element_type=F32)
    b = jnp.dot(y, wb_ref[...].astype(BF16), preferred_element_type=F32)
    o_ref[...] = (r_ref[...] + dr_ref[...].astype(F32)) + a * jax.nn.sigmoid(b)


def _glu(y, wa, wb, layer, resid, dresid, tm, tn):
    m, k = y.shape
    n = wa.shape[2]
    wspec = pl.BlockSpec((None, k, tn), lambda i, j: (layer, 0, j))
    tile = pl.BlockSpec((tm, tn), lambda i, j: (i, j))
    return pl.pallas_call(
        _glu_kernel,
        grid=(m // tm, n // tn),
        in_specs=[pl.BlockSpec((tm, k), lambda i, j: (i, 0), pipeline_mode=pl.Buffered(1)),
                  wspec, wspec, tile, tile],
        out_specs=tile,
        out_shape=jax.ShapeDtypeStruct((m, n), F32),
        compiler_params=_cparams(("parallel", "arbitrary")),
        name="glu",
    )(y, wa, wb, resid, dresid)


def _cmul(a_re, a_im, b_re, b_im):
    return a_re * b_re - a_im * b_im, a_re * b_im + a_im * b_re


def _ssm_prep_kernel(are_ref, aim_ref, ldt_ref, brr_ref, bii_ref, ca_ref, cb_ref, q_ref, cat_ref, pt_ref):
    t = SSM_CHUNK
    rows = are_ref.shape[0]
    l_re, l_im = are_ref[...], aim_ref[...]
    dt = jnp.exp(ldt_ref[...])
    mag = jnp.exp(l_re * dt)
    a_re, a_im = mag * jnp.cos(l_im * dt), mag * jnp.sin(l_im * dt)
    den = l_re * l_re + l_im * l_im
    n_re, n_im = a_re - 1.0, a_im
    k_re = (n_re * l_re + n_im * l_im) / den
    k_im = (n_im * l_re - n_re * l_im) / den
    bb_re, bb_im = _cmul(k_re, k_im, brr_ref[...], bii_ref[...])
    c_a, c_b = ca_ref[...], cb_ref[...]
    p_re, p_im = jnp.ones_like(a_re), jnp.zeros_like(a_re)
    for n in range(t):
        j = t - 1 - n
        q_re, q_im = _cmul(p_re, p_im, bb_re, bb_im)
        q_ref[0, j * rows:(j + 1) * rows, 0:LANES] = q_re.astype(q_ref.dtype)
        q_ref[0, j * rows:(j + 1) * rows, LANES:] = q_im.astype(q_ref.dtype)
        p_re, p_im = _cmul(p_re, p_im, a_re, a_im)
        cat_ref[0, :, n * rows:(n + 1) * rows] = (p_re * c_a + p_im * c_b).T.astype(cat_ref.dtype)
    half = lax.broadcasted_iota(jnp.int32, p_re.shape, 1) < (p_re.shape[1] // 2)
    pt_ref[...] = jnp.where(half, p_re, p_im)


def _ssm_prep(a_re, a_im, log_dt, b_re, b_im, c_re, c_im):
    g, p = a_re.shape
    c = b_re.shape[-1]
    t = SSM_CHUNK
    tiles = g // GROUPS_PER_TILE
    rows = g * c
    assert 2 * p == LANES
    dup = lambda v: jnp.concatenate([v, v], axis=-1)
    per_row = lambda v: dup(jnp.repeat(v, c, axis=0))
    bt_re = b_re.transpose(0, 2, 1).reshape(rows, p)
    bt_im = b_im.transpose(0, 2, 1).reshape(rows, p)
    cr, ci = c_re.reshape(rows, p), c_im.reshape(rows, p)
    args = (per_row(a_re), per_row(a_im), per_row(jnp.broadcast_to(log_dt[:, None], (g, p))),
            dup(bt_re), dup(bt_im),
            jnp.concatenate([cr, -ci], axis=-1), jnp.concatenate([-ci, -cr], axis=-1))
    blk = pl.BlockSpec((LANES, LANES), lambda i: (i, 0))
    q, cat, pt = pl.pallas_call(
        _ssm_prep_kernel,
        grid=(tiles,),
        in_specs=[blk] * 7,
        out_specs=[pl.BlockSpec((1, t * LANES, 2 * LANES), lambda i: (i, 0, 0)),
                   pl.BlockSpec((1, LANES, t * LANES), lambda i: (i, 0, 0)),
                   blk],
        out_shape=[jax.ShapeDtypeStruct((tiles, t * LANES, 2 * LANES), BF16),
                   jax.ShapeDtypeStruct((tiles, LANES, t * LANES), BF16),
                   jax.ShapeDtypeStruct((rows, LANES), F32)],
        compiler_params=_cparams(("parallel",)),
        name="ssm_prep",
    )(*args)
    pt = pt.reshape(tiles, GROUPS_PER_TILE, c, 2, p)[:, :, 0]
    a_t = pt.transpose(0, 2, 1, 3).reshape(tiles, 1, 2 * STATE_LANES)
    zero = jnp.zeros_like(cr)
    c_sel = jnp.concatenate([cr, zero, -ci, zero], axis=-1).astype(BF16)
    return q, cat, c_sel, a_t


def _gelu_tanh(x):
    return 0.5 * x * (1.0 + jnp.tanh(math.sqrt(2.0 / math.pi) * (x + 0.044715 * x * x * x)))


SSM_POS_PER_DOT = 4


def _ssm_kernel(x_ref, q_ref, cat_ref, ca_ref, at_ref, h0_ref, d_ref, y_ref, hf_ref,
                xs_ref, ys_ref, g_ref, hp_ref, bexp_ref, cexp_ref, *, n_prompt_chunks):
    t = SSM_CHUNK
    nq = SSM_POS_PER_DOT
    qw = nq * LANES
    nch = x_ref.shape[0] // t
    ns = nch - n_prompt_chunks
    sl = STATE_LANES
    ii = lambda shape, dim: lax.broadcasted_iota(jnp.int32, shape, dim)

    half = ii((qw, LANES), 1) // SSM_STATE
    row_g = (ii((qw, LANES), 0) % LANES) // SSM_GROUP
    for b in range(t // nq):
        rows = slice(b * qw, (b + 1) * qw)
        for lt in range(2 * sl // LANES):
            r, g0 = divmod(lt * LANES // SSM_STATE, GROUPS_PER_TILE)
            src = q_ref[0, rows, r * LANES:(r + 1) * LANES]
            bexp_ref[rows, lt * LANES:(lt + 1) * LANES] = jnp.where(row_g == g0 + half, src, jnp.zeros_like(src))
    lane_g = (ii((SSM_STATE, t * LANES), 1) % LANES) // SSM_GROUP
    for rb in range(2 * sl // SSM_STATE):
        r, g0 = divmod(rb, GROUPS_PER_TILE)
        src = cat_ref[0, r * SSM_STATE:(r + 1) * SSM_STATE, :]
        cexp_ref[rb * SSM_STATE:(rb + 1) * SSM_STATE, :] = jnp.where(lane_g == g0, src, jnp.zeros_like(src))
    kr = lax.dot_general(q_ref[0], ca_ref[...], (((1,), (1,)), ((), ())), preferred_element_type=F32)
    same = (ii((LANES, LANES), 0) // SSM_GROUP) == (ii((LANES, LANES), 1) // SSM_GROUP)
    krev = [jnp.where(same, kr[j * LANES:(j + 1) * LANES, :], 0.0).astype(BF16) for j in range(t)]

    xs_ref[...] = x_ref[...].astype(F32)
    us = [xs_ref[pl.ds(j, nch, stride=t), :] for j in range(t)]
    un = jnp.concatenate([u.astype(BF16) for u in us], axis=1)
    g_ref[...] = jnp.dot(un, bexp_ref[...], preferred_element_type=F32)

    at = at_ref[0]
    a_re, a_im = at[:, :sl], at[:, sl:]
    rid = ii((SUBLANES, sl), 0)
    pw = [(a_re, a_im)]
    for _ in range(SUBLANES - 1):
        pw.append(_cmul(pw[-1][0], pw[-1][1], a_re, a_im))
    car_re = jnp.concatenate([p[0] for p in pw], axis=0)
    car_im = jnp.concatenate([p[1] for p in pw], axis=0)
    levels = []
    s = 1
    while s < SUBLANES:
        levels.append((s, jnp.where(rid >= s, pw[s - 1][0], 0.0), jnp.where(rid >= s, pw[s - 1][1], 0.0)))
        s *= 2

    def block(b, carry):
        c_re, c_im = carry
        r0 = pl.multiple_of(b * SUBLANES, SUBLANES)
        g = g_ref[pl.ds(r0, SUBLANES), :]
        h_re, h_im = g[:, :sl], g[:, sl:]
        for s, l_re, l_im in levels:
            s_re, s_im = pltpu.roll(h_re, s, 0), pltpu.roll(h_im, s, 0)
            d_re, d_im = _cmul(l_re, l_im, s_re, s_im)
            h_re, h_im = h_re + d_re, h_im + d_im
        d_re, d_im = _cmul(car_re, car_im, c_re, c_im)
        h_re, h_im = h_re + d_re, h_im + d_im
        p_re = jnp.where(rid == 0, c_re, pltpu.roll(h_re, 1, 0))
        p_im = jnp.where(rid == 0, c_im, pltpu.roll(h_im, 1, 0))
        hp_ref[pl.ds(r0, SUBLANES), :] = jnp.concatenate([p_re, p_im], axis=1)
        return h_re[SUBLANES - 1:, :], h_im[SUBLANES - 1:, :]

    zero = jnp.zeros((1, sl), F32)
    e_re, e_im = lax.fori_loop(0, n_prompt_chunks // SUBLANES, block, (zero, zero))

    h0 = h0_ref[0]
    hp_ref[n_prompt_chunks:, :] = h0
    gs = g_ref[n_prompt_chunks:, :]
    s_re, s_im = _cmul(a_re, a_im, h0[:, :sl], h0[:, sl:])
    hf_ref[0, 0:ns, :] = jnp.concatenate([s_re + gs[:, :sl], s_im + gs[:, sl:]], axis=1)
    hf_ref[0, ns:, :] = jnp.broadcast_to(jnp.concatenate([e_re, e_im], axis=1), (hf_ref.shape[1] - ns, 2 * sl))

    hp = hp_ref[...].astype(BF16)
    dsk = d_ref[...]
    zblk = jnp.zeros((LANES, LANES), BF16)
    for b in range(t // nq):
        j_hi = (b + 1) * nq
        cols = []
        for j in range(b * nq, j_hi):
            cols.append(jnp.concatenate(krev[t - 1 - j:] + [zblk] * (j_hi - 1 - j), axis=0))
        kq = jnp.concatenate(cols, axis=1)
        y = jnp.dot(un[:, :j_hi * LANES], kq, preferred_element_type=F32)
        y = y + jnp.dot(hp, cexp_ref[:, b * qw:(b + 1) * qw], preferred_element_type=F32)
        for jj in range(nq):
            j = b * nq + jj
            ys_ref[pl.ds(j, nch, stride=t), :] = _gelu_tanh(y[:, jj * LANES:(jj + 1) * LANES] + dsk * us[j])
    y_ref[...] = ys_ref[...].astype(y_ref.dtype)


def _ssm(xn, tables, h0, d_skip, n_prompt_chunks):
    q, cat, c_sel, a_t = tables
    m, d = xn.shape
    tiles = d // LANES
    t = SSM_CHUNK
    nch = m // t
    ns = nch - n_prompt_chunks
    assert n_prompt_chunks % SUBLANES == 0 and ns % SUBLANES == 0 and t % SSM_POS_PER_DOT == 0
    hf_rows = 2 * ns
    tl = t * LANES
    s2 = 2 * STATE_LANES
    return pl.pallas_call(
        functools.partial(_ssm_kernel, n_prompt_chunks=n_prompt_chunks),
        grid=(tiles,),
        in_specs=[pl.BlockSpec((m, LANES), lambda i: (0, i)),
                  pl.BlockSpec((1, tl, 2 * LANES), lambda i: (i, 0, 0)),
                  pl.BlockSpec((1, LANES, tl), lambda i: (i, 0, 0)),
                  pl.BlockSpec((LANES, 2 * LANES), lambda i: (i, 0)),
                  pl.BlockSpec((1, 1, s2), lambda i: (i, 0, 0)),
                  pl.BlockSpec((1, ns, s2), lambda i: (i, 0, 0)),
                  pl.BlockSpec((1, LANES), lambda i: (0, i))],
        out_specs=[pl.BlockSpec((m, LANES), lambda i: (0, i)),
                   pl.BlockSpec((1, hf_rows, s2), lambda i: (i, 0, 0))],
        out_shape=[jax.ShapeDtypeStruct((m, d), BF16),
                   jax.ShapeDtypeStruct((tiles, hf_rows, s2), F32)],
        scratch_shapes=[pltpu.VMEM((m, LANES), F32),
                        pltpu.VMEM((m, LANES), F32),
                        pltpu.VMEM((nch, s2), F32),
                        pltpu.VMEM((nch, s2), F32),
                        pltpu.VMEM((tl, s2), BF16),
                        pltpu.VMEM((s2, tl), BF16)],
        compiler_params=_cparams(("parallel",)),
        name="ssm",
    )(xn, q, cat, c_sel, a_t, h0, d_skip.reshape(1, d))


def _state_to_tiles(h_re, h_im):
    b, g, p = h_re.shape
    tiles = g // GROUPS_PER_TILE
    re = h_re.reshape(b, tiles, STATE_LANES).transpose(1, 0, 2)
    im = h_im.reshape(b, tiles, STATE_LANES).transpose(1, 0, 2)
    return jnp.concatenate([re, im], axis=-1)


def _tiles_to_state(h):
    tiles, b, _ = h.shape
    re = h[..., :STATE_LANES].transpose(1, 0, 2).reshape(b, tiles * GROUPS_PER_TILE, SSM_STATE)
    im = h[..., STATE_LANES:].transpose(1, 0, 2).reshape(b, tiles * GROUPS_PER_TILE, SSM_STATE)
    return re, im


UP_ROWS, UP_COLS = 2080, 512
DOWN_ROWS, DOWN_COLS, DOWN_DEPTH = 1664, 1024, 2048
GLU_ROWS, GLU_COLS = 2080, 256
NORM_TILE = 416
POOL_TILE = 256


def kernel(x_prompt, x_sample, cache_pool, state_ssm_re, state_ssm_im, norm_mix, w_pool, pool_scale, ssm_a_re, ssm_a_im, ssm_log_dt, ssm_b_re, ssm_b_im, ssm_c_re, ssm_c_im, ssm_d, w_glu_a, w_glu_b, norm_ffn, w_up, w_down, norm_final):
    bp, lp, d = x_prompt.shape
    bs, ls, _ = x_sample.shape
    assert bp == 1 and ls == SSM_CHUNK and ls == HIST_PAD and lp % SSM_CHUNK == 0
    ms = bs * ls

    hist_s = jnp.pad(cache_pool[0], ((0, 0), (1, 0), (0, 0))).reshape(ms, d)
    x1, xn, hist_p, xn_s = _pool_layer(x_prompt[0], x_sample.reshape(ms, d), hist_s, norm_mix[0],
                                       w_pool[0].astype(BF16), pool_scale[0], norm_ffn[0], POOL_TILE, ls)
    pool_rows_p = hist_p[1:].reshape(1, 1, POOL_HIST, d)
    pool_rows_s = xn_s.reshape(bs, ls, d)[:, 1:].reshape(1, bs, POOL_HIST, d)

    h = _mlp_up(xn, w_up, 0, UP_ROWS, UP_COLS)
    dx1 = _mlp_down(h, w_down, 0, DOWN_ROWS, DOWN_COLS, DOWN_DEPTH)

    xn = _add_norm(x1, dx1, norm_mix[1], BF16, NORM_TILE)
    tables = _ssm_prep(ssm_a_re[0], ssm_a_im[0], ssm_log_dt[0], ssm_b_re[0], ssm_b_im[0],
                       ssm_c_re[0], ssm_c_im[0])
    h0 = _state_to_tiles(state_ssm_re[0], state_ssm_im[0])
    y, hf = _ssm(xn, tables, h0, ssm_d[0], lp // SSM_CHUNK)
    x3 = _glu(y, w_glu_a, w_glu_b, 0, x1, dx1, GLU_ROWS, GLU_COLS)
    re_s, im_s = _tiles_to_state(hf[:, :bs])
    re_p, im_p = _tiles_to_state(hf[:, bs:bs + 1])

    xn = _rmsnorm(x3, norm_ffn[1], BF16, NORM_TILE)
    h = _mlp_up(xn, w_up, 1, UP_ROWS, UP_COLS)
    dx3 = _mlp_down(h, w_down, 1, DOWN_ROWS, DOWN_COLS, DOWN_DEPTH)
    y_p = _add_norm(x3, dx3, norm_final, F32, 512, 0, lp).reshape(1, lp, d)
    y_s = _add_norm(x3, dx3, norm_final, F32, ms, lp, ms).reshape(bs, ls, d)
    return (y_p, y_s, pool_rows_p, pool_rows_s, re_p[None], im_p[None], re_s[None], im_s[None])
```

```python
import functools
import math

import jax
import jax.numpy as jnp
from jax import lax
from jax.experimental import pallas as pl
from jax.experimental.pallas import tpu as pltpu

F32 = jnp.float32
BF16 = jnp.bfloat16

EPS = 1e-6
PAST_LEN = 1024
POOL_WINDOWS = (2, 4, 8, 16)
POOL_HIST = max(POOL_WINDOWS) - 1
HIST_PAD = POOL_HIST + 1
SSM_GROUP = 16
SSM_STATE = 64
SSM_CHUNK = 16
LANES = 128
SUBLANES = 8
GROUPS_PER_TILE = LANES // SSM_GROUP
STATE_LANES = GROUPS_PER_TILE * SSM_STATE
VMEM_LIMIT = 56 * 1024 * 1024


def _cparams(sem):
    return pltpu.CompilerParams(dimension_semantics=sem, vmem_limit_bytes=VMEM_LIMIT)


def _rms(x, g):
    return x * lax.rsqrt(jnp.mean(x * x, axis=-1, keepdims=True) + EPS) * g


def _rmsnorm_kernel(x_ref, g_ref, o_ref):
    o_ref[...] = _rms(x_ref[...], g_ref[...]).astype(o_ref.dtype)


def _rmsnorm(x, g, out_dtype, tm):
    m, d = x.shape
    return pl.pallas_call(
        _rmsnorm_kernel,
        grid=(m // tm,),
        in_specs=[pl.BlockSpec((tm, d), lambda i: (i, 0)),
                  pl.BlockSpec((1, d), lambda i: (0, 0))],
        out_specs=pl.BlockSpec((tm, d), lambda i: (i, 0)),
        out_shape=jax.ShapeDtypeStruct((m, d), out_dtype),
        compiler_params=_cparams(("parallel",)),
        name="rmsnorm",
    )(x, g.reshape(1, d))


def _add_norm_kernel(x_ref, dx_ref, g_ref, o_ref):
    o_ref[...] = _rms(x_ref[...] + dx_ref[...].astype(F32), g_ref[...]).astype(o_ref.dtype)


def _add_norm(x, dx, g, out_dtype, tm, row0=0, rows=None):
    m, d = x.shape
    rows = m - row0 if rows is None else rows
    assert rows % tm == 0 and row0 % tm == 0
    off = row0 // tm
    src = pl.BlockSpec((tm, d), lambda i: (i + off, 0))
    return pl.pallas_call(
        _add_norm_kernel,
        grid=(rows // tm,),
        in_specs=[src, src, pl.BlockSpec((1, d), lambda i: (0, 0))],
        out_specs=pl.BlockSpec((tm, d), lambda i: (i, 0)),
        out_shape=jax.ShapeDtypeStruct((rows, d), out_dtype),
        compiler_params=_cparams(("parallel",)),
        name="add_norm",
    )(x, dx, g.reshape(1, d))


def _pool_finish(x, a, s_scaled, gi, pg, w_ref, sc_ref, x1_ref, rows):
    sl = slice(gi * pg, (gi + 1) * pg)
    pooled = s_scaled - a
    mixed = jnp.dot(pooled.astype(BF16), w_ref[gi], preferred_element_type=F32)
    x1 = x[:, sl] + mixed * sc_ref[:, sl]
    x1_ref[0:rows, sl] = x1
    return jnp.sum(x1 * x1, axis=-1, keepdims=True)


def _pool_kernel(xp_ref, xs_ref, hs_ref, gm_ref, w_ref, sc_ref, gf_ref,
                 x1_ref, xnf_ref, hist_ref, xns_ref, ext_ref, *, tm, n_prompt_tiles, seq):
    i = pl.program_id(0)
    d = xp_ref.shape[1]
    pg = d // len(POOL_WINDOWS)

    @pl.when(i == 0)
    def _():
        ext_ref[0:HIST_PAD, :] = jnp.zeros((HIST_PAD, d), F32)

    @pl.when(i < n_prompt_tiles)
    def _():
        x = xp_ref[...]
        ext_ref[HIST_PAD:, :] = _rms(x, gm_ref[...])
        pos = i * tm + lax.broadcasted_iota(jnp.int32, (tm, 1), 0)
        ss = jnp.zeros((tm, 1), F32)
        for gi, w in enumerate(POOL_WINDOWS):
            a = ext_ref[:, gi * pg:(gi + 1) * pg]
            s = a
            span = 1
            while span < w:
                s = s + pltpu.roll(s, span, 0)
                span *= 2
            inv_cnt = 1.0 / jnp.minimum(w, pos + 1).astype(F32)
            ss = ss + _pool_finish(x, a[HIST_PAD:, :], s[HIST_PAD:, :] * inv_cnt, gi, pg, w_ref, sc_ref, x1_ref, tm)
        inv = lax.rsqrt(ss / d + EPS)
        xnf_ref[...] = (x1_ref[...] * inv * gf_ref[...]).astype(xnf_ref.dtype)
        hist_ref[...] = ext_ref[tm:, :]
        ext_ref[0:HIST_PAD, :] = ext_ref[tm:, :]

    @pl.when(i == n_prompt_tiles)
    def _():
        ms = xs_ref.shape[0]
        x = xs_ref[...]
        xns_ref[...] = _rms(x, gm_ref[...])
        t_in_seq = lax.broadcasted_iota(jnp.int32, (ms, 1), 0) % seq
        ss = jnp.zeros((ms, 1), F32)
        for gi, w in enumerate(POOL_WINDOWS):
            sl = slice(gi * pg, (gi + 1) * pg)
            a = xns_ref[:, sl]
            h = hs_ref[:, sl]
            s = a
            for lag in range(1, w):
                cur = pltpu.roll(a, lag, 0)
                old = pltpu.roll(h, (ms - HIST_PAD + lag) % ms, 0)
                s = s + jnp.where(t_in_seq >= lag, cur, old)
            ss = ss + _pool_finish(x, a, s / float(w), gi, pg, w_ref, sc_ref, x1_ref, ms)
        inv = lax.rsqrt(ss / d + EPS)
        xnf_ref[0:ms, :] = (x1_ref[0:ms, :] * inv * gf_ref[...]).astype(xnf_ref.dtype)


def _pool_layer(x_prompt, x_sample, hist_sample, g_mix, w_pool_bf, scale, g_ffn, tm, seq):
    l, d = x_prompt.shape
    ms = x_sample.shape[0]
    assert l % tm == 0 and ms <= tm and PAST_LEN >= POOL_HIST
    nt = l // tm
    ng, pg, _ = w_pool_bf.shape
    vec = pl.BlockSpec((1, d), lambda i: (0, 0))
    whole = pl.BlockSpec((ms, d), lambda i: (0, 0))
    tile = pl.BlockSpec((tm, d), lambda i: (i, 0))
    return pl.pallas_call(
        functools.partial(_pool_kernel, tm=tm, n_prompt_tiles=nt, seq=seq),
        grid=(nt + 1,),
        in_specs=[pl.BlockSpec((tm, d), lambda i: (jnp.minimum(i, nt - 1), 0)), whole, whole, vec,
                  pl.BlockSpec((ng, pg, pg), lambda i: (0, 0, 0)), vec, vec],
        out_specs=[tile, tile, pl.BlockSpec((HIST_PAD, d), lambda i: (0, 0)), whole],
        out_shape=[jax.ShapeDtypeStruct((l + ms, d), F32),
                   jax.ShapeDtypeStruct((l + ms, d), BF16),
                   jax.ShapeDtypeStruct((HIST_PAD, d), F32),
                   jax.ShapeDtypeStruct((ms, d), F32)],
        scratch_shapes=[pltpu.VMEM((HIST_PAD + tm, d), F32)],
        compiler_params=_cparams(("arbitrary",)),
        name="pool_layer",
    )(x_prompt, x_sample, hist_sample, g_mix.reshape(1, d), w_pool_bf, scale.reshape(1, d), g_ffn.reshape(1, d))


def _up_kernel(x_ref, w_ref, o_ref):
    acc = jnp.dot(x_ref[...], w_ref[...].astype(BF16), preferred_element_type=F32)
    h = jnp.maximum(acc, 0.0)
    o_ref[...] = (h * h).astype(o_ref.dtype)


def _mlp_up(xn, w, layer, tm, tn):
    m, k = xn.shape
    n = w.shape[2]
    return pl.pallas_call(
        _up_kernel,
        grid=(m // tm, n // tn),
        in_specs=[pl.BlockSpec((tm, k), lambda i, j: (i, 0), pipeline_mode=pl.Buffered(1)),
                  pl.BlockSpec((None, k, tn), lambda i, j: (layer, 0, j))],
        out_specs=pl.BlockSpec((tm, tn), lambda i, j: (i, j)),
        out_shape=jax.ShapeDtypeStruct((m, n), BF16),
        compiler_params=_cparams(("parallel", "arbitrary")),
        name="mlp_up",
    )(xn, w)


def _down_kernel(h_ref, w_ref, o_ref, acc_ref):
    kk = pl.program_id(2)

    @pl.when(kk == 0)
    def _():
        acc_ref[...] = jnp.zeros_like(acc_ref)

    acc_ref[...] += jnp.dot(h_ref[...], w_ref[...].astype(BF16), preferred_element_type=F32)

    @pl.when(kk == pl.num_programs(2) - 1)
    def _():
        o_ref[...] = acc_ref[...].astype(o_ref.dtype)


def _mlp_down(h, w, layer, tm, tn, tk):
    m, k = h.shape
    n = w.shape[2]
    return pl.pallas_call(
        _down_kernel,
        grid=(m // tm, n // tn, k // tk),
        in_specs=[pl.BlockSpec((tm, tk), lambda i, j, kk: (i, kk)),
                  pl.BlockSpec((None, tk, tn), lambda i, j, kk: (layer, kk, j))],
        out_specs=pl.BlockSpec((tm, tn), lambda i, j, kk: (i, j)),
        out_shape=jax.ShapeDtypeStruct((m, n), BF16),
        scratch_shapes=[pltpu.VMEM((tm, tn), F32)],
        compiler_params=_cparams(("parallel", "parallel", "arbitrary")),
        name="mlp_down",
    )(h, w)


def _glu_kernel(y_ref, wa_ref, wb_ref, r_ref, dr_ref, o_ref):
    y = y_ref[...]
    a = jnp.dot(y, wa_ref[...].astype(BF16), preferred_element_type=F32)
    b = jnp.dot(y, wb_ref[...].astype(BF16), preferred_element_type=F32)
    o_ref[...] = (r_ref[...] + dr_ref[...].astype(F32)) + a * jax.nn.sigmoid(b)


def _glu(y, wa, wb, layer, resid, dresid, tm, tn):
    m, k = y.shape
    n = wa.shape[2]
    wspec = pl.BlockSpec((None, k, tn), lambda i, j: (layer, 0, j))
    tile = pl.BlockSpec((tm, tn), lambda i, j: (i, j))
    return pl.pallas_call(
        _glu_kernel,
        grid=(m // tm, n // tn),
        in_specs=[pl.BlockSpec((tm, k), lambda i, j: (i, 0), pipeline_mode=pl.Buffered(1)),
                  wspec, wspec, tile, tile],
        out_specs=tile,
        out_shape=jax.ShapeDtypeStruct((m, n), F32),
        compiler_params=_cparams(("parallel", "arbitrary")),
        name="glu",
    )(y, wa, wb, resid, dresid)


def _cmul(a_re, a_im, b_re, b_im):
    return a_re * b_re - a_im * b_im, a_re * b_im + a_im * b_re


def _ssm_prep_kernel(are_ref, aim_ref, ldt_ref, brr_ref, bii_ref, ca_ref, cb_ref, q_ref, cat_ref, pt_ref):
    t = SSM_CHUNK
    rows = are_ref.shape[0]
    l_re, l_im = are_ref[...], aim_ref[...]
    dt = jnp.exp(ldt_ref[...])
    mag = jnp.exp(l_re * dt)
    a_re, a_im = mag * jnp.cos(l_im * dt), mag * jnp.sin(l_im * dt)
    den = l_re * l_re + l_im * l_im
    n_re, n_im = a_re - 1.0, a_im
    k_re = (n_re * l_re + n_im * l_im) / den
    k_im = (n_im * l_re - n_re * l_im) / den
    bb_re, bb_im = _cmul(k_re, k_im, brr_ref[...], bii_ref[...])
    c_a, c_b = ca_ref[...], cb_ref[...]
    p_re, p_im = jnp.ones_like(a_re), jnp.zeros_like(a_re)
    for n in range(t):
        j = t - 1 - n
        q_re, q_im = _cmul(p_re, p_im, bb_re, bb_im)
        q_ref[0, j * rows:(j + 1) * rows, 0:LANES] = q_re.astype(q_ref.dtype)
        q_ref[0, j * rows:(j + 1) * rows, LANES:] = q_im.astype(q_ref.dtype)
        p_re, p_im = _cmul(p_re, p_im, a_re, a_im)
        cat_ref[0, :, n * rows:(n + 1) * rows] = (p_re * c_a + p_im * c_b).T.astype(cat_ref.dtype)
    half = lax.broadcasted_iota(jnp.int32, p_re.shape, 1) < (p_re.shape[1] // 2)
    pt_ref[...] = jnp.where(half, p_re, p_im)


def _ssm_prep(a_re, a_im, log_dt, b_re, b_im, c_re, c_im):
    g, p = a_re.shape
    c = b_re.shape[-1]
    t = SSM_CHUNK
    tiles = g // GROUPS_PER_TILE
    rows = g * c
    assert 2 * p == LANES
    dup = lambda v: jnp.concatenate([v, v], axis=-1)
    per_row = lambda v: dup(jnp.repeat(v, c, axis=0))
    bt_re = b_re.transpose(0, 2, 1).reshape(rows, p)
    bt_im = b_im.transpose(0, 2, 1).reshape(rows, p)
    cr, ci = c_re.reshape(rows, p), c_im.reshape(rows, p)
    args = (per_row(a_re), per_row(a_im), per_row(jnp.broadcast_to(log_dt[:, None], (g, p))),
            dup(bt_re), dup(bt_im),
            jnp.concatenate([cr, -ci], axis=-1), jnp.concatenate([-ci, -cr], axis=-1))
    blk = pl.BlockSpec((LANES, LANES), lambda i: (i, 0))
    q, cat, pt = pl.pallas_call(
        _ssm_prep_kernel,
        grid=(tiles,),
        in_specs=[blk] * 7,
        out_specs=[pl.BlockSpec((1, t * LANES, 2 * LANES), lambda i: (i, 0, 0)),
                   pl.BlockSpec((1, LANES, t * LANES), lambda i: (i, 0, 0)),
                   blk],
        out_shape=[jax.ShapeDtypeStruct((tiles, t * LANES, 2 * LANES), BF16),
                   jax.ShapeDtypeStruct((tiles, LANES, t * LANES), BF16),
                   jax.ShapeDtypeStruct((rows, LANES), F32)],
        compiler_params=_cparams(("parallel",)),
        name="ssm_prep",
    )(*args)
    pt = pt.reshape(tiles, GROUPS_PER_TILE, c, 2, p)[:, :, 0]
    a_t = pt.transpose(0, 2, 1, 3).reshape(tiles, 1, 2 * STATE_LANES)
    zero = jnp.zeros_like(cr)
    c_sel = jnp.concatenate([cr, zero, -ci, zero], axis=-1).astype(BF16)
    return q, cat, c_sel, a_t


def _gelu_tanh(x):
    return 0.5 * x * (1.0 + jnp.tanh(math.sqrt(2.0 / math.pi) * (x + 0.044715 * x * x * x)))


SSM_POS_PER_DOT = 4


def _ssm_kernel(x_ref, q_ref, cat_ref, ca_ref, at_ref, h0_ref, d_ref, y_ref, hf_ref,
                xs_ref, ys_ref, g_ref, hp_ref, bexp_ref, cexp_ref, yi_ref, *, n_prompt_chunks):
    t = SSM_CHUNK
    nq = SSM_POS_PER_DOT
    qw = nq * LANES
    nch = x_ref.shape[0] // t
    ns = nch - n_prompt_chunks
    sl = STATE_LANES
    ii = lambda shape, dim: lax.broadcasted_iota(jnp.int32, shape, dim)

    half = ii((qw, LANES), 1) // SSM_STATE
    row_g = (ii((qw, LANES), 0) % LANES) // SSM_GROUP
    for b in range(t // nq):
        rows = slice(b * qw, (b + 1) * qw)
        for lt in range(2 * sl // LANES):
            r, g0 = divmod(lt * LANES // SSM_STATE, GROUPS_PER_TILE)
            src = q_ref[0, rows, r * LANES:(r + 1) * LANES]
            bexp_ref[rows, lt * LANES:(lt + 1) * LANES] = jnp.where(row_g == g0 + half, src, jnp.zeros_like(src))
    lane_g = (ii((SSM_STATE, t * LANES), 1) % LANES) // SSM_GROUP
    for rb in range(2 * sl // SSM_STATE):
        r, g0 = divmod(rb, GROUPS_PER_TILE)
        src = cat_ref[0, r * SSM_STATE:(r + 1) * SSM_STATE, :]
        cexp_ref[rb * SSM_STATE:(rb + 1) * SSM_STATE, :] = jnp.where(lane_g == g0, src, jnp.zeros_like(src))
    kr = lax.dot_general(q_ref[0], ca_ref[...], (((1,), (1,)), ((), ())), preferred_element_type=F32)
    same = (ii((LANES, LANES), 0) // SSM_GROUP) == (ii((LANES, LANES), 1) // SSM_GROUP)
    krev = [jnp.where(same, kr[j * LANES:(j + 1) * LANES, :], 0.0).astype(BF16) for j in range(t)]

    xs_ref[...] = x_ref[...].astype(F32)
    us = [xs_ref[pl.ds(j, nch, stride=t), :] for j in range(t)]
    un = jnp.concatenate([u.astype(BF16) for u in us], axis=1)
    g_ref[...] = jnp.dot(un, bexp_ref[...], preferred_element_type=F32)

    zblk = jnp.zeros((LANES, LANES), BF16)
    for b in range(t // nq):
        j_hi = (b + 1) * nq
        cols = []
        for j in range(b * nq, j_hi):
            cols.append(jnp.concatenate(krev[t - 1 - j:] + [zblk] * (j_hi - 1 - j), axis=0))
        kq = jnp.concatenate(cols, axis=1)
        yi_ref[:, b * qw:(b + 1) * qw] = jnp.dot(un[:, :j_hi * LANES], kq, preferred_element_type=F32)

    at = at_ref[0]
    a_re, a_im = at[:, :sl], at[:, sl:]
    rid = ii((SUBLANES, sl), 0)
    pw = [(a_re, a_im)]
    for _ in range(SUBLANES - 1):
        pw.append(_cmul(pw[-1][0], pw[-1][1], a_re, a_im))
    car_re = jnp.concatenate([p[0] for p in pw], axis=0)
    car_im = jnp.concatenate([p[1] for p in pw], axis=0)
    levels = []
    s = 1
    while s < SUBLANES:
        levels.append((s, jnp.where(rid >= s, pw[s - 1][0], 0.0), jnp.where(rid >= s, pw[s - 1][1], 0.0)))
        s *= 2

    def block(b, carry):
        c_re, c_im = carry
        r0 = b * SUBLANES
        g = g_ref[pl.ds(r0, SUBLANES), :]
        h_re, h_im = g[:, :sl], g[:, sl:]
        for s, l_re, l_im in levels:
            s_re, s_im = pltpu.roll(h_re, s, 0), pltpu.roll(h_im, s, 0)
            d_re, d_im = _cmul(l_re, l_im, s_re, s_im)
            h_re, h_im = h_re + d_re, h_im + d_im
        d_re, d_im = _cmul(car_re, car_im, c_re, c_im)
        h_re, h_im = h_re + d_re, h_im + d_im
        p_re = jnp.where(rid == 0, c_re, pltpu.roll(h_re, 1, 0))
        p_im = jnp.where(rid == 0, c_im, pltpu.roll(h_im, 1, 0))
        hp_ref[pl.ds(r0, SUBLANES), :] = jnp.concatenate([p_re, p_im], axis=1)
        return h_re[SUBLANES - 1:, :], h_im[SUBLANES - 1:, :]

    e_re = e_im = jnp.zeros((1, sl), F32)
    for b in range(n_prompt_chunks // SUBLANES):
        e_re, e_im = block(b, (e_re, e_im))

    h0 = h0_ref[0]
    hp_ref[n_prompt_chunks:, :] = h0
    gs = g_ref[n_prompt_chunks:, :]
    s_re, s_im = _cmul(a_re, a_im, h0[:, :sl], h0[:, sl:])
    hf_ref[0, 0:ns, :] = jnp.concatenate([s_re + gs[:, :sl], s_im + gs[:, sl:]], axis=1)
    hf_ref[0, ns:, :] = jnp.broadcast_to(jnp.concatenate([e_re, e_im], axis=1), (hf_ref.shape[1] - ns, 2 * sl))

    hp = hp_ref[...].astype(BF16)
    dsk = d_ref[...]
    for b in range(t // nq):
        y = yi_ref[:, b * qw:(b + 1) * qw]
        y = y + jnp.dot(hp, cexp_ref[:, b * qw:(b + 1) * qw], preferred_element_type=F32)
        for jj in range(nq):
            j = b * nq + jj
            ys_ref[pl.ds(j, nch, stride=t), :] = _gelu_tanh(y[:, jj * LANES:(jj + 1) * LANES] + dsk * us[j])
    y_ref[...] = ys_ref[...].astype(y_ref.dtype)


def _ssm(xn, tables, h0, d_skip, n_prompt_chunks):
    q, cat, c_sel, a_t = tables
    m, d = xn.shape
    tiles = d // LANES
    t = SSM_CHUNK
    nch = m // t
    ns = nch - n_prompt_chunks
    assert n_prompt_chunks % SUBLANES == 0 and ns % SUBLANES == 0 and t % SSM_POS_PER_DOT == 0
    hf_rows = 2 * ns
    tl = t * LANES
    s2 = 2 * STATE_LANES
    return pl.pallas_call(
        functools.partial(_ssm_kernel, n_prompt_chunks=n_prompt_chunks),
        grid=(tiles,),
        in_specs=[pl.BlockSpec((m, LANES), lambda i: (0, i)),
                  pl.BlockSpec((1, tl, 2 * LANES), lambda i: (i, 0, 0)),
                  pl.BlockSpec((1, LANES, tl), lambda i: (i, 0, 0)),
                  pl.BlockSpec((LANES, 2 * LANES), lambda i: (i, 0)),
                  pl.BlockSpec((1, 1, s2), lambda i: (i, 0, 0)),
                  pl.BlockSpec((1, ns, s2), lambda i: (i, 0, 0)),
                  pl.BlockSpec((1, LANES), lambda i: (0, i))],
        out_specs=[pl.BlockSpec((m, LANES), lambda i: (0, i)),
                   pl.BlockSpec((1, hf_rows, s2), lambda i: (i, 0, 0))],
        out_shape=[jax.ShapeDtypeStruct((m, d), BF16),
                   jax.ShapeDtypeStruct((tiles, hf_rows, s2), F32)],
        scratch_shapes=[pltpu.VMEM((m, LANES), F32),
                        pltpu.VMEM((m, LANES), F32),
                        pltpu.VMEM((nch, s2), F32),
                        pltpu.VMEM((nch, s2), F32),
                        pltpu.VMEM((tl, s2), BF16),
                        pltpu.VMEM((s2, tl), BF16),
                        pltpu.VMEM((nch, tl), F32)],
        compiler_params=_cparams(("parallel",)),
        name="ssm",
    )(xn, q, cat, c_sel, a_t, h0, d_skip.reshape(1, d))


def _state_to_tiles(h_re, h_im):
    b, g, p = h_re.shape
    tiles = g // GROUPS_PER_TILE
    re = h_re.reshape(b, tiles, STATE_LANES).transpose(1, 0, 2)
    im = h_im.reshape(b, tiles, STATE_LANES).transpose(1, 0, 2)
    return jnp.concatenate([re, im], axis=-1)


def _tiles_to_state(h):
    tiles, b, _ = h.shape
    re = h[..., :STATE_LANES].transpose(1, 0, 2).reshape(b, tiles * GROUPS_PER_TILE, SSM_STATE)
    im = h[..., STATE_LANES:].transpose(1, 0, 2).reshape(b, tiles * GROUPS_PER_TILE, SSM_STATE)
    return re, im


UP_ROWS, UP_COLS = 2080, 512
DOWN_ROWS, DOWN_COLS, DOWN_DEPTH = 1664, 1024, 2048
GLU_ROWS, GLU_COLS = 2080, 256
NORM_TILE = 416
POOL_TILE = 256


def kernel(x_prompt, x_sample, cache_pool, state_ssm_re, state_ssm_im, norm_mix, w_pool, pool_scale, ssm_a_re, ssm_a_im, ssm_log_dt, ssm_b_re, ssm_b_im, ssm_c_re, ssm_c_im, ssm_d, w_glu_a, w_glu_b, norm_ffn, w_up, w_down, norm_final):
    bp, lp, d = x_prompt.shape
    bs, ls, _ = x_sample.shape
    assert bp == 1 and ls == SSM_CHUNK and ls == HIST_PAD and lp % SSM_CHUNK == 0
    ms = bs * ls

    hist_s = jnp.pad(cache_pool[0], ((0, 0), (1, 0), (0, 0))).reshape(ms, d)
    x1, xn, hist_p, xn_s = _pool_layer(x_prompt[0], x_sample.reshape(ms, d), hist_s, norm_mix[0],
                                       w_pool[0].astype(BF16), pool_scale[0], norm_ffn[0], POOL_TILE, ls)
    pool_rows_p = hist_p[1:].reshape(1, 1, POOL_HIST, d)
    pool_rows_s = xn_s.reshape(bs, ls, d)[:, 1:].reshape(1, bs, POOL_HIST, d)

    h = _mlp_up(xn, w_up, 0, UP_ROWS, UP_COLS)
    dx1 = _mlp_down(h, w_down, 0, DOWN_ROWS, DOWN_COLS, DOWN_DEPTH)

    xn = _add_norm(x1, dx1, norm_mix[1], BF16, NORM_TILE)
    tables = _ssm_prep(ssm_a_re[0], ssm_a_im[0], ssm_log_dt[0], ssm_b_re[0], ssm_b_im[0],
                       ssm_c_re[0], ssm_c_im[0])
    h0 = _state_to_tiles(state_ssm_re[0], state_ssm_im[0])
    y, hf = _ssm(xn, tables, h0, ssm_d[0], lp // SSM_CHUNK)
    x3 = _glu(y, w_glu_a, w_glu_b, 0, x1, dx1, GLU_ROWS, GLU_COLS)
    re_s, im_s = _tiles_to_state(hf[:, :bs])
    re_p, im_p = _tiles_to_state(hf[:, bs:bs + 1])

    xn = _rmsnorm(x3, norm_ffn[1], BF16, NORM_TILE)
    h = _mlp_up(xn, w_up, 1, UP_ROWS, UP_COLS)
    dx3 = _mlp_down(h, w_down, 1, DOWN_ROWS, DOWN_COLS, DOWN_DEPTH)
    y_p = _add_norm(x3, dx3, norm_final, F32, 512, 0, lp).reshape(1, lp, d)
    y_s = _add_norm(x3, dx3, norm_final, F32, ms, lp, ms).reshape(bs, ls, d)
    return (y_p, y_s, pool_rows_p, pool_rows_s, re_p[None], im_p[None], re_s[None], im_s[None])
```

```python
import functools
import math

import jax
import jax.numpy as jnp
from jax import lax
from jax.experimental import pallas as pl
from jax.experimental.pallas import tpu as pltpu

F32 = jnp.float32
BF16 = jnp.bfloat16

EPS = 1e-6
PAST_LEN = 1024
POOL_WINDOWS = (2, 4, 8, 16)
POOL_HIST = max(POOL_WINDOWS) - 1
HIST_PAD = POOL_HIST + 1
SSM_GROUP = 16
SSM_STATE = 64
SSM_CHUNK = 16
LANES = 128
SUBLANES = 8
GROUPS_PER_TILE = LANES // SSM_GROUP
STATE_LANES = GROUPS_PER_TILE * SSM_STATE
VMEM_LIMIT = 60 * 1024 * 1024


def _cparams(sem):
    return pltpu.CompilerParams(dimension_semantics=sem, vmem_limit_bytes=VMEM_LIMIT)


def _rms(x, g):
    return x * lax.rsqrt(jnp.mean(x * x, axis=-1, keepdims=True) + EPS) * g


def _rmsnorm_kernel(x_ref, g_ref, o_ref):
    o_ref[...] = _rms(x_ref[...], g_ref[...]).astype(o_ref.dtype)


def _rmsnorm(x, g, out_dtype, tm):
    m, d = x.shape
    return pl.pallas_call(
        _rmsnorm_kernel,
        grid=(m // tm,),
        in_specs=[pl.BlockSpec((tm, d), lambda i: (i, 0)),
                  pl.BlockSpec((1, d), lambda i: (0, 0))],
        out_specs=pl.BlockSpec((tm, d), lambda i: (i, 0)),
        out_shape=jax.ShapeDtypeStruct((m, d), out_dtype),
        compiler_params=_cparams(("parallel",)),
        name="rmsnorm",
    )(x, g.reshape(1, d))


def _add_norm_kernel(x_ref, dx_ref, g_ref, o_ref):
    o_ref[...] = _rms(x_ref[...] + dx_ref[...].astype(F32), g_ref[...]).astype(o_ref.dtype)


def _add_norm(x, dx, g, out_dtype, tm, row0=0, rows=None):
    m, d = x.shape
    rows = m - row0 if rows is None else rows
    assert rows % tm == 0 and row0 % tm == 0
    off = row0 // tm
    src = pl.BlockSpec((tm, d), lambda i: (i + off, 0))
    return pl.pallas_call(
        _add_norm_kernel,
        grid=(rows // tm,),
        in_specs=[src, src, pl.BlockSpec((1, d), lambda i: (0, 0))],
        out_specs=pl.BlockSpec((tm, d), lambda i: (i, 0)),
        out_shape=jax.ShapeDtypeStruct((rows, d), out_dtype),
        compiler_params=_cparams(("parallel",)),
        name="add_norm",
    )(x, dx, g.reshape(1, d))


def _pool_finish(x, a, s_scaled, gi, pg, w_ref, sc_ref, x1_ref, rows):
    sl = slice(gi * pg, (gi + 1) * pg)
    pooled = s_scaled - a
    mixed = jnp.dot(pooled.astype(BF16), w_ref[gi], preferred_element_type=F32)
    x1 = x[:, sl] + mixed * sc_ref[:, sl]
    x1_ref[0:rows, sl] = x1
    return jnp.sum(x1 * x1, axis=-1, keepdims=True)


def _pool_kernel(xp_ref, xs_ref, hs_ref, gm_ref, w_ref, sc_ref, gf_ref,
                 x1_ref, xnf_ref, hist_ref, xns_ref, ext_ref, *, tm, n_prompt_tiles, seq):
    i = pl.program_id(0)
    d = xp_ref.shape[1]
    pg = d // len(POOL_WINDOWS)

    @pl.when(i == 0)
    def _():
        ext_ref[0:HIST_PAD, :] = jnp.zeros((HIST_PAD, d), F32)

    @pl.when(i < n_prompt_tiles)
    def _():
        x = xp_ref[...]
        ext_ref[HIST_PAD:, :] = _rms(x, gm_ref[...])
        pos = i * tm + lax.broadcasted_iota(jnp.int32, (tm, 1), 0)
        ss = jnp.zeros((tm, 1), F32)
        for gi, w in enumerate(POOL_WINDOWS):
            a = ext_ref[:, gi * pg:(gi + 1) * pg]
            s = a
            span = 1
            while span < w:
                s = s + pltpu.roll(s, span, 0)
                span *= 2
            inv_cnt = 1.0 / jnp.minimum(w, pos + 1).astype(F32)
            ss = ss + _pool_finish(x, a[HIST_PAD:, :], s[HIST_PAD:, :] * inv_cnt, gi, pg, w_ref, sc_ref, x1_ref, tm)
        inv = lax.rsqrt(ss / d + EPS)
        xnf_ref[...] = (x1_ref[...] * inv * gf_ref[...]).astype(xnf_ref.dtype)
        hist_ref[...] = ext_ref[tm:, :]
        ext_ref[0:HIST_PAD, :] = ext_ref[tm:, :]

    @pl.when(i == n_prompt_tiles)
    def _():
        ms = xs_ref.shape[0]
        x = xs_ref[...]
        xns_ref[...] = _rms(x, gm_ref[...])
        t_in_seq = lax.broadcasted_iota(jnp.int32, (ms, 1), 0) % seq
        ss = jnp.zeros((ms, 1), F32)
        for gi, w in enumerate(POOL_WINDOWS):
            sl = slice(gi * pg, (gi + 1) * pg)
            a = xns_ref[:, sl]
            h = hs_ref[:, sl]
            s = a
            for lag in range(1, w):
                cur = pltpu.roll(a, lag, 0)
                old = pltpu.roll(h, (ms - HIST_PAD + lag) % ms, 0)
                s = s + jnp.where(t_in_seq >= lag, cur, old)
            ss = ss + _pool_finish(x, a, s / float(w), gi, pg, w_ref, sc_ref, x1_ref, ms)
        inv = lax.rsqrt(ss / d + EPS)
        xnf_ref[0:ms, :] = (x1_ref[0:ms, :] * inv * gf_ref[...]).astype(xnf_ref.dtype)


def _pool_layer(x_prompt, x_sample, hist_sample, g_mix, w_pool_bf, scale, g_ffn, tm, seq):
    l, d = x_prompt.shape
    ms = x_sample.shape[0]
    assert l % tm == 0 and ms <= tm and PAST_LEN >= POOL_HIST
    nt = l // tm
    ng, pg, _ = w_pool_bf.shape
    vec = pl.BlockSpec((1, d), lambda i: (0, 0))
    whole = pl.BlockSpec((ms, d), lambda i: (0, 0))
    tile = pl.BlockSpec((tm, d), lambda i: (i, 0))
    return pl.pallas_call(
        functools.partial(_pool_kernel, tm=tm, n_prompt_tiles=nt, seq=seq),
        grid=(nt + 1,),
        in_specs=[pl.BlockSpec((tm, d), lambda i: (jnp.minimum(i, nt - 1), 0)), whole, whole, vec,
                  pl.BlockSpec((ng, pg, pg), lambda i: (0, 0, 0)), vec, vec],
        out_specs=[tile, tile, pl.BlockSpec((HIST_PAD, d), lambda i: (0, 0)), whole],
        out_shape=[jax.ShapeDtypeStruct((l + ms, d), F32),
                   jax.ShapeDtypeStruct((l + ms, d), BF16),
                   jax.ShapeDtypeStruct((HIST_PAD, d), F32),
                   jax.ShapeDtypeStruct((ms, d), F32)],
        scratch_shapes=[pltpu.VMEM((HIST_PAD + tm, d), F32)],
        compiler_params=_cparams(("arbitrary",)),
        name="pool_layer",
    )(x_prompt, x_sample, hist_sample, g_mix.reshape(1, d), w_pool_bf, scale.reshape(1, d), g_ffn.reshape(1, d))


def _up_kernel(x_ref, w_ref, o_ref):
    acc = jnp.dot(x_ref[...], w_ref[...].astype(BF16), preferred_element_type=F32)
    h = jnp.maximum(acc, 0.0)
    o_ref[...] = (h * h).astype(o_ref.dtype)


def _mlp_up(xn, w, layer, tm, tn):
    m, k = xn.shape
    n = w.shape[2]
    return pl.pallas_call(
        _up_kernel,
        grid=(m // tm, n // tn),
        in_specs=[pl.BlockSpec((tm, k), lambda i, j: (i, 0)),
                  pl.BlockSpec((None, k, tn), lambda i, j: (layer, 0, j))],
        out_specs=pl.BlockSpec((tm, tn), lambda i, j: (i, j)),
        out_shape=jax.ShapeDtypeStruct((m, n), BF16),
        compiler_params=_cparams(("parallel", "arbitrary")),
        name="mlp_up",
    )(xn, w)


def _down_kernel(h_ref, w_ref, o_ref, acc_ref):
    kk = pl.program_id(2)

    @pl.when(kk == 0)
    def _():
        acc_ref[...] = jnp.zeros_like(acc_ref)

    acc_ref[...] += jnp.dot(h_ref[...], w_ref[...].astype(BF16), preferred_element_type=F32)

    @pl.when(kk == pl.num_programs(2) - 1)
    def _():
        o_ref[...] = acc_ref[...].astype(o_ref.dtype)


def _mlp_down(h, w, layer, tm, tn, tk):
    m, k = h.shape
    n = w.shape[2]
    return pl.pallas_call(
        _down_kernel,
        grid=(m // tm, n // tn, k // tk),
        in_specs=[pl.BlockSpec((tm, tk), lambda i, j, kk: (i, kk)),
                  pl.BlockSpec((None, tk, tn), lambda i, j, kk: (layer, kk, j))],
        out_specs=pl.BlockSpec((tm, tn), lambda i, j, kk: (i, j)),
        out_shape=jax.ShapeDtypeStruct((m, n), BF16),
        scratch_shapes=[pltpu.VMEM((tm, tn), F32)],
        compiler_params=_cparams(("parallel", "parallel", "arbitrary")),
        name="mlp_down",
    )(h, w)


def _glu_kernel(y_ref, wa_ref, wb_ref, r_ref, dr_ref, o_ref):
    y = y_ref[...]
    a = jnp.dot(y, wa_ref[...].astype(BF16), preferred_element_type=F32)
    b = jnp.dot(y, wb_ref[...].astype(BF16), preferred_element_type=F32)
    o_ref[...] = (r_ref[...] + dr_ref[...].astype(F32)) + a * jax.nn.sigmoid(b)


def _glu(y, wa, wb, layer, resid, dresid, tm, tn):
    m, k = y.shape
    n = wa.shape[2]
    wspec = pl.BlockSpec((None, k, tn), lambda i, j: (layer, 0, j))
    tile = pl.BlockSpec((tm, tn), lambda i, j: (i, j))
    return pl.pallas_call(
        _glu_kernel,
        grid=(m // tm, n // tn),
        in_specs=[pl.BlockSpec((tm, k), lambda i, j: (i, 0), pipeline_mode=pl.Buffered(1)),
                  wspec, wspec, tile, tile],
        out_specs=tile,
        out_shape=jax.ShapeDtypeStruct((m, n), F32),
        compiler_params=_cparams(("parallel", "arbitrary")),
        name="glu",
    )(y, wa, wb, resid, dresid)


def _cmul(a_re, a_im, b_re, b_im):
    return a_re * b_re - a_im * b_im, a_re * b_im + a_im * b_re


def _ssm_prep_kernel(are_ref, aim_ref, ldt_ref, brr_ref, bii_ref, ca_ref, cb_ref, q_ref, cat_ref, pt_ref):
    t = SSM_CHUNK
    rows = brr_ref.shape[0]

    def per_channel(v):
        return jnp.concatenate([jnp.broadcast_to(v[g:g + 1, :], (SSM_GROUP, v.shape[1])) for g in range(v.shape[0])],
                               axis=0)

    l_re, l_im = are_ref[...], aim_ref[...]
    dt = jnp.exp(ldt_ref[...])
    mag = jnp.exp(l_re * dt)
    a_re, a_im = mag * jnp.cos(l_im * dt), mag * jnp.sin(l_im * dt)
    den = l_re * l_re + l_im * l_im
    n_re, n_im = a_re - 1.0, a_im
    k_re = (n_re * l_re + n_im * l_im) / den
    k_im = (n_im * l_re - n_re * l_im) / den
    bb_re, bb_im = _cmul(per_channel(k_re), per_channel(k_im), brr_ref[...], bii_ref[...])
    c_a, c_b = ca_ref[...], cb_ref[...]
    p_re, p_im = jnp.ones_like(a_re), jnp.zeros_like(a_re)
    for n in range(t):
        j = t - 1 - n
        q_re, q_im = _cmul(per_channel(p_re), per_channel(p_im), bb_re, bb_im)
        q_ref[0, j * rows:(j + 1) * rows, 0:LANES] = q_re.astype(q_ref.dtype)
        q_ref[0, j * rows:(j + 1) * rows, LANES:] = q_im.astype(q_ref.dtype)
        p_re, p_im = _cmul(p_re, p_im, a_re, a_im)
        cat = per_channel(p_re) * c_a + per_channel(p_im) * c_b
        cat_ref[0, :, n * rows:(n + 1) * rows] = cat.T.astype(cat_ref.dtype)
    half = lax.broadcasted_iota(jnp.int32, p_re.shape, 1) < (p_re.shape[1] // 2)
    pt_ref[...] = jnp.where(half, p_re, p_im)


def _ssm_prep(a_re, a_im, log_dt, b_re, b_im, c_re, c_im):
    g, p = a_re.shape
    c = b_re.shape[-1]
    t = SSM_CHUNK
    tiles = g // GROUPS_PER_TILE
    rows = g * c
    assert 2 * p == LANES and c == SSM_GROUP
    dup = lambda v: jnp.concatenate([v, v], axis=-1)
    bt_re = b_re.transpose(0, 2, 1).reshape(rows, p)
    bt_im = b_im.transpose(0, 2, 1).reshape(rows, p)
    cr, ci = c_re.reshape(rows, p), c_im.reshape(rows, p)
    args = (dup(a_re), dup(a_im), dup(jnp.broadcast_to(log_dt[:, None], (g, p))),
            dup(bt_re), dup(bt_im),
            jnp.concatenate([cr, -ci], axis=-1), jnp.concatenate([-ci, -cr], axis=-1))
    grp = pl.BlockSpec((GROUPS_PER_TILE, LANES), lambda i: (i, 0))
    blk = pl.BlockSpec((LANES, LANES), lambda i: (i, 0))
    q, cat, pt = pl.pallas_call(
        _ssm_prep_kernel,
        grid=(tiles,),
        in_specs=[grp] * 3 + [blk] * 4,
        out_specs=[pl.BlockSpec((1, t * LANES, 2 * LANES), lambda i: (i, 0, 0)),
                   pl.BlockSpec((1, LANES, t * LANES), lambda i: (i, 0, 0)),
                   grp],
        out_shape=[jax.ShapeDtypeStruct((tiles, t * LANES, 2 * LANES), BF16),
                   jax.ShapeDtypeStruct((tiles, LANES, t * LANES), BF16),
                   jax.ShapeDtypeStruct((g, LANES), F32)],
        compiler_params=_cparams(("parallel",)),
        name="ssm_prep",
    )(*args)
    a_t = pt.reshape(tiles, GROUPS_PER_TILE, 2, p).transpose(0, 2, 1, 3).reshape(tiles, 1, 2 * STATE_LANES)
    zero = jnp.zeros_like(cr)
    c_sel = jnp.concatenate([cr, zero, -ci, zero], axis=-1).astype(BF16)
    return q, cat, c_sel, a_t


def _gelu_tanh(x):
    return 0.5 * x * (1.0 + jnp.tanh(math.sqrt(2.0 / math.pi) * (x + 0.044715 * x * x * x)))


SSM_POS_PER_DOT = 4


def _ssm_kernel(x_ref, q_ref, cat_ref, ca_ref, at_ref, h0_ref, d_ref, y_ref, hf_ref,
                xs_ref, ys_ref, g_ref, hp_ref, bexp_ref, cexp_ref, yi_ref, *, n_prompt_chunks):
    t = SSM_CHUNK
    nq = SSM_POS_PER_DOT
    qw = nq * LANES
    nch = x_ref.shape[0] // t
    ns = nch - n_prompt_chunks
    sl = STATE_LANES
    ii = lambda shape, dim: lax.broadcasted_iota(jnp.int32, shape, dim)

    half = ii((qw, LANES), 1) // SSM_STATE
    row_g = (ii((qw, LANES), 0) % LANES) // SSM_GROUP
    for b in range(t // nq):
        rows = slice(b * qw, (b + 1) * qw)
        for lt in range(2 * sl // LANES):
            r, g0 = divmod(lt * LANES // SSM_STATE, GROUPS_PER_TILE)
            src = q_ref[0, rows, r * LANES:(r + 1) * LANES]
            bexp_ref[rows, lt * LANES:(lt + 1) * LANES] = jnp.where(row_g == g0 + half, src, jnp.zeros_like(src))
    lane_g = (ii((SSM_STATE, t * LANES), 1) % LANES) // SSM_GROUP
    for rb in range(2 * sl // SSM_STATE):
        r, g0 = divmod(rb, GROUPS_PER_TILE)
        src = cat_ref[0, r * SSM_STATE:(r + 1) * SSM_STATE, :]
        cexp_ref[rb * SSM_STATE:(rb + 1) * SSM_STATE, :] = jnp.where(lane_g == g0, src, jnp.zeros_like(src))
    kr = lax.dot_general(q_ref[0], ca_ref[...], (((1,), (1,)), ((), ())), preferred_element_type=F32)
    same = (ii((LANES, LANES), 0) // SSM_GROUP) == (ii((LANES, LANES), 1) // SSM_GROUP)
    krev = [jnp.where(same, kr[j * LANES:(j + 1) * LANES, :], 0.0).astype(BF16) for j in range(t)]

    xs_ref[...] = x_ref[...].astype(F32)
    us = [xs_ref[pl.ds(j, nch, stride=t), :] for j in range(t)]
    un = jnp.concatenate([u.astype(BF16) for u in us], axis=1)
    g_ref[...] = jnp.dot(un, bexp_ref[...], preferred_element_type=F32)

    zblk = jnp.zeros((LANES, LANES), BF16)
    for b in range(t // nq):
        j_hi = (b + 1) * nq
        cols = []
        for j in range(b * nq, j_hi):
            cols.append(jnp.concatenate(krev[t - 1 - j:] + [zblk] * (j_hi - 1 - j), axis=0))
        kq = jnp.concatenate(cols, axis=1)
        yi_ref[:, b * qw:(b + 1) * qw] = jnp.dot(un[:, :j_hi * LANES], kq, preferred_element_type=F32)

    at = at_ref[0]
    a_re, a_im = at[:, :sl], at[:, sl:]
    rid = ii((SUBLANES, sl), 0)
    pw = [(a_re, a_im)]
    for _ in range(SUBLANES - 1):
        pw.append(_cmul(pw[-1][0], pw[-1][1], a_re, a_im))
    car_re = jnp.concatenate([p[0] for p in pw], axis=0)
    car_im = jnp.concatenate([p[1] for p in pw], axis=0)
    levels = []
    s = 1
    while s < SUBLANES:
        levels.append((s, jnp.where(rid >= s, pw[s - 1][0], 0.0), jnp.where(rid >= s, pw[s - 1][1], 0.0)))
        s *= 2

    def block(b, carry):
        c_re, c_im = carry
        r0 = b * SUBLANES
        g = g_ref[pl.ds(r0, SUBLANES), :]
        h_re, h_im = g[:, :sl], g[:, sl:]
        for s, l_re, l_im in levels:
            s_re, s_im = pltpu.roll(h_re, s, 0), pltpu.roll(h_im, s, 0)
            d_re, d_im = _cmul(l_re, l_im, s_re, s_im)
            h_re, h_im = h_re + d_re, h_im + d_im
        d_re, d_im = _cmul(car_re, car_im, c_re, c_im)
        h_re, h_im = h_re + d_re, h_im + d_im
        p_re = jnp.where(rid == 0, c_re, pltpu.roll(h_re, 1, 0))
        p_im = jnp.where(rid == 0, c_im, pltpu.roll(h_im, 1, 0))
        hp_ref[pl.ds(r0, SUBLANES), :] = jnp.concatenate([p_re, p_im], axis=1)
        return h_re[SUBLANES - 1:, :], h_im[SUBLANES - 1:, :]

    e_re = e_im = jnp.zeros((1, sl), F32)
    for b in range(n_prompt_chunks // SUBLANES):
        e_re, e_im = block(b, (e_re, e_im))

    h0 = h0_ref[0]
    hp_ref[n_prompt_chunks:, :] = h0
    gs = g_ref[n_prompt_chunks:, :]
    s_re, s_im = _cmul(a_re, a_im, h0[:, :sl], h0[:, sl:])
    hf_ref[0, 0:ns, :] = jnp.concatenate([s_re + gs[:, :sl], s_im + gs[:, sl:]], axis=1)
    hf_ref[0, ns:, :] = jnp.broadcast_to(jnp.concatenate([e_re, e_im], axis=1), (hf_ref.shape[1] - ns, 2 * sl))

    hp = hp_ref[...].astype(BF16)
    dsk = d_ref[...]
    for b in range(t // nq):
        y = yi_ref[:, b * qw:(b + 1) * qw]
        y = y + jnp.dot(hp, cexp_ref[:, b * qw:(b + 1) * qw], preferred_element_type=F32)
        for jj in range(nq):
            j = b * nq + jj
            ys_ref[pl.ds(j, nch, stride=t), :] = _gelu_tanh(y[:, jj * LANES:(jj + 1) * LANES] + dsk * us[j])
    y_ref[...] = ys_ref[...].astype(y_ref.dtype)


def _ssm(xn, tables, h0, d_skip, n_prompt_chunks):
    q, cat, c_sel, a_t = tables
    m, d = xn.shape
    tiles = d // LANES
    t = SSM_CHUNK
    nch = m // t
    ns = nch - n_prompt_chunks
    assert n_prompt_chunks % SUBLANES == 0 and ns % SUBLANES == 0 and t % SSM_POS_PER_DOT == 0
    hf_rows = 2 * ns
    tl = t * LANES
    s2 = 2 * STATE_LANES
    return pl.pallas_call(
        functools.partial(_ssm_kernel, n_prompt_chunks=n_prompt_chunks),
        grid=(tiles,),
        in_specs=[pl.BlockSpec((m, LANES), lambda i: (0, i)),
                  pl.BlockSpec((1, tl, 2 * LANES), lambda i: (i, 0, 0)),
                  pl.BlockSpec((1, LANES, tl), lambda i: (i, 0, 0)),
                  pl.BlockSpec((LANES, 2 * LANES), lambda i: (i, 0)),
                  pl.BlockSpec((1, 1, s2), lambda i: (i, 0, 0)),
                  pl.BlockSpec((1, ns, s2), lambda i: (i, 0, 0)),
                  pl.BlockSpec((1, LANES), lambda i: (0, i))],
        out_specs=[pl.BlockSpec((m, LANES), lambda i: (0, i)),
                   pl.BlockSpec((1, hf_rows, s2), lambda i: (i, 0, 0))],
        out_shape=[jax.ShapeDtypeStruct((m, d), BF16),
                   jax.ShapeDtypeStruct((tiles, hf_rows, s2), F32)],
        scratch_shapes=[pltpu.VMEM((m, LANES), F32),
                        pltpu.VMEM((m, LANES), F32),
                        pltpu.VMEM((nch, s2), F32),
                        pltpu.VMEM((nch, s2), F32),
                        pltpu.VMEM((tl, s2), BF16),
                        pltpu.VMEM((s2, tl), BF16),
                        pltpu.VMEM((nch, tl), F32)],
        compiler_params=_cparams(("parallel",)),
        name="ssm",
    )(xn, q, cat, c_sel, a_t, h0, d_skip.reshape(1, d))


def _state_to_tiles(h_re, h_im):
    b, g, p = h_re.shape
    tiles = g // GROUPS_PER_TILE
    re = h_re.reshape(b, tiles, STATE_LANES).transpose(1, 0, 2)
    im = h_im.reshape(b, tiles, STATE_LANES).transpose(1, 0, 2)
    return jnp.concatenate([re, im], axis=-1)


def _tiles_to_state(h):
    tiles, b, _ = h.shape
    re = h[..., :STATE_LANES].transpose(1, 0, 2).reshape(b, tiles * GROUPS_PER_TILE, SSM_STATE)
    im = h[..., STATE_LANES:].transpose(1, 0, 2).reshape(b, tiles * GROUPS_PER_TILE, SSM_STATE)
    return re, im


UP_ROWS, UP_COLS = 2080, 512
DOWN_ROWS, DOWN_COLS, DOWN_DEPTH = 1664, 1024, 2048
GLU_ROWS, GLU_COLS = 2080, 256
NORM_TILE = 416
POOL_TILE = 256


def kernel(x_prompt, x_sample, cache_pool, state_ssm_re, state_ssm_im, norm_mix, w_pool, pool_scale, ssm_a_re, ssm_a_im, ssm_log_dt, ssm_b_re, ssm_b_im, ssm_c_re, ssm_c_im, ssm_d, w_glu_a, w_glu_b, norm_ffn, w_up, w_down, norm_final):
    bp, lp, d = x_prompt.shape
    bs, ls, _ = x_sample.shape
    assert bp == 1 and ls == SSM_CHUNK and ls == HIST_PAD and lp % SSM_CHUNK == 0
    ms = bs * ls

    hist_s = jnp.pad(cache_pool[0], ((0, 0), (1, 0), (0, 0))).reshape(ms, d)
    x1, xn, hist_p, xn_s = _pool_layer(x_prompt[0], x_sample.reshape(ms, d), hist_s, norm_mix[0],
                                       w_pool[0].astype(BF16), pool_scale[0], norm_ffn[0], POOL_TILE, ls)
    pool_rows_p = hist_p[1:].reshape(1, 1, POOL_HIST, d)
    pool_rows_s = xn_s.reshape(bs, ls, d)[:, 1:].reshape(1, bs, POOL_HIST, d)

    h = _mlp_up(xn, w_up, 0, UP_ROWS, UP_COLS)
    dx1 = _mlp_down(h, w_down, 0, DOWN_ROWS, DOWN_COLS, DOWN_DEPTH)

    xn = _add_norm(x1, dx1, norm_mix[1], BF16, NORM_TILE)
    tables = _ssm_prep(ssm_a_re[0], ssm_a_im[0], ssm_log_dt[0], ssm_b_re[0], ssm_b_im[0],
                       ssm_c_re[0], ssm_c_im[0])
    h0 = _state_to_tiles(state_ssm_re[0], state_ssm_im[0])
    y, hf = _ssm(xn, tables, h0, ssm_d[0], lp // SSM_CHUNK)
    x3 = _glu(y, w_glu_a, w_glu_b, 0, x1, dx1, GLU_ROWS, GLU_COLS)
    re_s, im_s = _tiles_to_state(hf[:, :bs])
    re_p, im_p = _tiles_to_state(hf[:, bs:bs + 1])

    xn = _rmsnorm(x3, norm_ffn[1], BF16, NORM_TILE)
    h = _mlp_up(xn, w_up, 1, UP_ROWS, UP_COLS)
    dx3 = _mlp_down(h, w_down, 1, DOWN_ROWS, DOWN_COLS, DOWN_DEPTH)
    y_p = _add_norm(x3, dx3, norm_final, F32, 512, 0, lp).reshape(1, lp, d)
    y_s = _add_norm(x3, dx3, norm_final, F32, ms, lp, ms).reshape(bs, ls, d)
    return (y_p, y_s, pool_rows_p, pool_rows_s, re_p[None], im_p[None], re_s[None], im_s[None])
```

```python
import functools
import math

import jax
import jax.numpy as jnp
from jax import lax
from jax.experimental import pallas as pl
from jax.experimental.pallas import tpu as pltpu

F32 = jnp.float32
BF16 = jnp.bfloat16

EPS = 1e-6
PAST_LEN = 1024
POOL_WINDOWS = (2, 4, 8, 16)
POOL_HIST = max(POOL_WINDOWS) - 1
HIST_PAD = POOL_HIST + 1
SSM_GROUP = 16
SSM_STATE = 64
SSM_CHUNK = 16
LANES = 128
SUBLANES = 8
GROUPS_PER_TILE = LANES // SSM_GROUP
STATE_LANES = GROUPS_PER_TILE * SSM_STATE
VMEM_LIMIT = 56 * 1024 * 1024
VMEM_LIMIT_UP = 60 * 1024 * 1024


def _cparams(sem, vmem_limit=VMEM_LIMIT):
    return pltpu.CompilerParams(dimension_semantics=sem, vmem_limit_bytes=vmem_limit)


def _rms(x, g):
    return x * lax.rsqrt(jnp.mean(x * x, axis=-1, keepdims=True) + EPS) * g


def _rmsnorm_kernel(x_ref, g_ref, o_ref):
    o_ref[...] = _rms(x_ref[...], g_ref[...]).astype(o_ref.dtype)


def _rmsnorm(x, g, out_dtype, tm):
    m, d = x.shape
    return pl.pallas_call(
        _rmsnorm_kernel,
        grid=(m // tm,),
        in_specs=[pl.BlockSpec((tm, d), lambda i: (i, 0)),
                  pl.BlockSpec((1, d), lambda i: (0, 0))],
        out_specs=pl.BlockSpec((tm, d), lambda i: (i, 0)),
        out_shape=jax.ShapeDtypeStruct((m, d), out_dtype),
        compiler_params=_cparams(("parallel",)),
        name="rmsnorm",
    )(x, g.reshape(1, d))


def _add_norm_kernel(x_ref, dx_ref, g_ref, o_ref):
    o_ref[...] = _rms(x_ref[...] + dx_ref[...].astype(F32), g_ref[...]).astype(o_ref.dtype)


def _add_norm(x, dx, g, out_dtype, tm, row0=0, rows=None):
    m, d = x.shape
    rows = m - row0 if rows is None else rows
    assert rows % tm == 0 and row0 % tm == 0
    off = row0 // tm
    src = pl.BlockSpec((tm, d), lambda i: (i + off, 0))
    return pl.pallas_call(
        _add_norm_kernel,
        grid=(rows // tm,),
        in_specs=[src, src, pl.BlockSpec((1, d), lambda i: (0, 0))],
        out_specs=pl.BlockSpec((tm, d), lambda i: (i, 0)),
        out_shape=jax.ShapeDtypeStruct((rows, d), out_dtype),
        compiler_params=_cparams(("parallel",)),
        name="add_norm",
    )(x, dx, g.reshape(1, d))


def _pool_finish(x, a, s_scaled, gi, pg, w_ref, sc_ref, x1_ref, rows):
    sl = slice(gi * pg, (gi + 1) * pg)
    pooled = s_scaled - a
    mixed = jnp.dot(pooled.astype(BF16), w_ref[gi], preferred_element_type=F32)
    x1 = x[:, sl] + mixed * sc_ref[:, sl]
    x1_ref[0:rows, sl] = x1
    return jnp.sum(x1 * x1, axis=-1, keepdims=True)


def _pool_kernel(xp_ref, xs_ref, hs_ref, gm_ref, wf_ref, sc_ref, gf_ref,
                 x1_ref, xnf_ref, hist_ref, xns_ref, ext_ref, w_ref, *, tm, n_prompt_tiles, seq):
    i = pl.program_id(0)
    d = xp_ref.shape[1]
    pg = d // len(POOL_WINDOWS)

    @pl.when(i == 0)
    def _():
        ext_ref[0:HIST_PAD, :] = jnp.zeros((HIST_PAD, d), F32)
        for gi in range(len(POOL_WINDOWS)):
            w_ref[gi] = wf_ref[gi].astype(BF16)

    @pl.when(i < n_prompt_tiles)
    def _():
        x = xp_ref[...]
        ext_ref[HIST_PAD:, :] = _rms(x, gm_ref[...])
        pos = i * tm + lax.broadcasted_iota(jnp.int32, (tm, 1), 0)
        ss = jnp.zeros((tm, 1), F32)
        for gi, w in enumerate(POOL_WINDOWS):
            a = ext_ref[:, gi * pg:(gi + 1) * pg]
            s = a
            span = 1
            while span < w:
                s = s + pltpu.roll(s, span, 0)
                span *= 2
            inv_cnt = 1.0 / jnp.minimum(w, pos + 1).astype(F32)
            ss = ss + _pool_finish(x, a[HIST_PAD:, :], s[HIST_PAD:, :] * inv_cnt, gi, pg, w_ref, sc_ref, x1_ref, tm)
        inv = lax.rsqrt(ss / d + EPS)
        xnf_ref[...] = (x1_ref[...] * inv * gf_ref[...]).astype(xnf_ref.dtype)
        hist_ref[...] = ext_ref[tm:, :]
        ext_ref[0:HIST_PAD, :] = ext_ref[tm:, :]

    @pl.when(i == n_prompt_tiles)
    def _():
        ms = xs_ref.shape[0]
        x = xs_ref[...]
        xns_ref[...] = _rms(x, gm_ref[...])
        t_in_seq = lax.broadcasted_iota(jnp.int32, (ms, 1), 0) % seq
        ss = jnp.zeros((ms, 1), F32)
        for gi, w in enumerate(POOL_WINDOWS):
            sl = slice(gi * pg, (gi + 1) * pg)
            a = xns_ref[:, sl]
            h = hs_ref[:, sl]
            s = a
            for lag in range(1, w):
                cur = pltpu.roll(a, lag, 0)
                old = pltpu.roll(h, (ms - HIST_PAD + lag) % ms, 0)
                s = s + jnp.where(t_in_seq >= lag, cur, old)
            ss = ss + _pool_finish(x, a, s / float(w), gi, pg, w_ref, sc_ref, x1_ref, ms)
        inv = lax.rsqrt(ss / d + EPS)
        xnf_ref[0:ms, :] = (x1_ref[0:ms, :] * inv * gf_ref[...]).astype(xnf_ref.dtype)


def _pool_layer(x_prompt, x_sample, hist_sample, g_mix, w_pool, scale, g_ffn, tm, seq):
    l, d = x_prompt.shape
    ms = x_sample.shape[0]
    assert l % tm == 0 and ms <= tm and PAST_LEN >= POOL_HIST
    nt = l // tm
    ng, pg, _ = w_pool.shape
    vec = pl.BlockSpec((1, d), lambda i: (0, 0))
    whole = pl.BlockSpec((ms, d), lambda i: (0, 0))
    tile = pl.BlockSpec((tm, d), lambda i: (i, 0))
    return pl.pallas_call(
        functools.partial(_pool_kernel, tm=tm, n_prompt_tiles=nt, seq=seq),
        grid=(nt + 1,),
        in_specs=[pl.BlockSpec((tm, d), lambda i: (jnp.minimum(i, nt - 1), 0)), whole, whole, vec,
                  pl.BlockSpec((ng, pg, pg), lambda i: (0, 0, 0), pipeline_mode=pl.Buffered(1)), vec, vec],
        out_specs=[tile, tile, pl.BlockSpec((HIST_PAD, d), lambda i: (0, 0)), whole],
        out_shape=[jax.ShapeDtypeStruct((l + ms, d), F32),
                   jax.ShapeDtypeStruct((l + ms, d), BF16),
                   jax.ShapeDtypeStruct((HIST_PAD, d), F32),
                   jax.ShapeDtypeStruct((ms, d), F32)],
        scratch_shapes=[pltpu.VMEM((HIST_PAD + tm, d), F32), pltpu.VMEM((ng, pg, pg), BF16)],
        compiler_params=_cparams(("arbitrary",)),
        name="pool_layer",
    )(x_prompt, x_sample, hist_sample, g_mix.reshape(1, d), w_pool, scale.reshape(1, d), g_ffn.reshape(1, d))


def _up_kernel(x_ref, w_ref, o_ref):
    acc = jnp.dot(x_ref[...], w_ref[...].astype(BF16), preferred_element_type=F32)
    h = jnp.maximum(acc, 0.0)
    o_ref[...] = (h * h).astype(o_ref.dtype)


def _mlp_up(xn, w, layer, tm, tn):
    m, k = xn.shape
    n = w.shape[2]
    return pl.pallas_call(
        _up_kernel,
        grid=(m // tm, n // tn),
        in_specs=[pl.BlockSpec((tm, k), lambda i, j: (i, 0)),
                  pl.BlockSpec((None, k, tn), lambda i, j: (layer, 0, j))],
        out_specs=pl.BlockSpec((tm, tn), lambda i, j: (i, j)),
        out_shape=jax.ShapeDtypeStruct((m, n), BF16),
        compiler_params=_cparams(("parallel", "arbitrary"), VMEM_LIMIT_UP),
        name="mlp_up",
    )(xn, w)


def _down_kernel(h_ref, w_ref, o_ref, acc_ref):
    kk = pl.program_id(2)

    @pl.when(kk == 0)
    def _():
        acc_ref[...] = jnp.zeros_like(acc_ref)

    acc_ref[...] += jnp.dot(h_ref[...], w_ref[...].astype(BF16), preferred_element_type=F32)

    @pl.when(kk == pl.num_programs(2) - 1)
    def _():
        o_ref[...] = acc_ref[...].astype(o_ref.dtype)


def _mlp_down(h, w, layer, tm, tn, tk):
    m, k = h.shape
    n = w.shape[2]
    return pl.pallas_call(
        _down_kernel,
        grid=(m // tm, n // tn, k // tk),
        in_specs=[pl.BlockSpec((tm, tk), lambda i, j, kk: (i, kk)),
                  pl.BlockSpec((None, tk, tn), lambda i, j, kk: (layer, kk, j))],
        out_specs=pl.BlockSpec((tm, tn), lambda i, j, kk: (i, j)),
        out_shape=jax.ShapeDtypeStruct((m, n), BF16),
        scratch_shapes=[pltpu.VMEM((tm, tn), F32)],
        compiler_params=_cparams(("parallel", "parallel", "arbitrary")),
        name="mlp_down",
    )(h, w)


def _glu_kernel(y_ref, wa_ref, wb_ref, r_ref, dr_ref, o_ref):
    y = y_ref[...]
    a = jnp.dot(y, wa_ref[...].astype(BF16), preferred_element_type=F32)
    b = jnp.dot(y, wb_ref[...].astype(BF16), preferred_element_type=F32)
    o_ref[...] = (r_ref[...] + dr_ref[...].astype(F32)) + a * jax.nn.sigmoid(b)


def _glu(y, wa, wb, layer, resid, dresid, tm, tn):
    m, k = y.shape
    n = wa.shape[2]
    wspec = pl.BlockSpec((None, k, tn), lambda i, j: (layer, 0, j))
    tile = pl.BlockSpec((tm, tn), lambda i, j: (i, j))
    return pl.pallas_call(
        _glu_kernel,
        grid=(m // tm, n // tn),
        in_specs=[pl.BlockSpec((tm, k), lambda i, j: (i, 0), pipeline_mode=pl.Buffered(1)),
                  wspec, wspec, tile, tile],
        out_specs=tile,
        out_shape=jax.ShapeDtypeStruct((m, n), F32),
        compiler_params=_cparams(("parallel", "arbitrary")),
        name="glu",
    )(y, wa, wb, resid, dresid)


def _cmul(a_re, a_im, b_re, b_im):
    return a_re * b_re - a_im * b_im, a_re * b_im + a_im * b_re


def _ssm_prep_kernel(are_ref, aim_ref, ldt_ref, brr_ref, bii_ref, ca_ref, cb_ref, q_ref, cat_ref, pt_ref):
    t = SSM_CHUNK
    rows = brr_ref.shape[0]

    def per_channel(v):
        return jnp.concatenate([jnp.broadcast_to(v[g:g + 1, :], (SSM_GROUP, v.shape[1])) for g in range(v.shape[0])],
                               axis=0)

    l_re, l_im = are_ref[...], aim_ref[...]
    dt = jnp.exp(ldt_ref[...])
    mag = jnp.exp(l_re * dt)
    a_re, a_im = mag * jnp.cos(l_im * dt), mag * jnp.sin(l_im * dt)
    den = l_re * l_re + l_im * l_im
    n_re, n_im = a_re - 1.0, a_im
    k_re = (n_re * l_re + n_im * l_im) / den
    k_im = (n_im * l_re - n_re * l_im) / den
    bb_re, bb_im = _cmul(per_channel(k_re), per_channel(k_im), brr_ref[...], bii_ref[...])
    c_a, c_b = ca_ref[...], cb_ref[...]
    p_re, p_im = jnp.ones_like(a_re), jnp.zeros_like(a_re)
    for n in range(t):
        j = t - 1 - n
        q_re, q_im = _cmul(per_channel(p_re), per_channel(p_im), bb_re, bb_im)
        q_ref[0, j * rows:(j + 1) * rows, 0:LANES] = q_re.astype(q_ref.dtype)
        q_ref[0, j * rows:(j + 1) * rows, LANES:] = q_im.astype(q_ref.dtype)
        p_re, p_im = _cmul(p_re, p_im, a_re, a_im)
        cat = per_channel(p_re) * c_a + per_channel(p_im) * c_b
        cat_ref[0, :, n * rows:(n + 1) * rows] = cat.T.astype(cat_ref.dtype)
    half = lax.broadcasted_iota(jnp.int32, p_re.shape, 1) < (p_re.shape[1] // 2)
    pt_ref[...] = jnp.where(half, p_re, p_im)


def _ssm_prep(a_re, a_im, log_dt, b_re, b_im, c_re, c_im):
    g, p = a_re.shape
    c = b_re.shape[-1]
    t = SSM_CHUNK
    tiles = g // GROUPS_PER_TILE
    rows = g * c
    assert 2 * p == LANES and c == SSM_GROUP
    dup = lambda v: jnp.concatenate([v, v], axis=-1)
    bt_re = b_re.transpose(0, 2, 1).reshape(rows, p)
    bt_im = b_im.transpose(0, 2, 1).reshape(rows, p)
    cr, ci = c_re.reshape(rows, p), c_im.reshape(rows, p)
    args = (dup(a_re), dup(a_im), dup(jnp.broadcast_to(log_dt[:, None], (g, p))),
            dup(bt_re), dup(bt_im),
            jnp.concatenate([cr, -ci], axis=-1), jnp.concatenate([-ci, -cr], axis=-1))
    grp = pl.BlockSpec((GROUPS_PER_TILE, LANES), lambda i: (i, 0))
    blk = pl.BlockSpec((LANES, LANES), lambda i: (i, 0))
    q, cat, pt = pl.pallas_call(
        _ssm_prep_kernel,
        grid=(tiles,),
        in_specs=[grp] * 3 + [blk] * 4,
        out_specs=[pl.BlockSpec((1, t * LANES, 2 * LANES), lambda i: (i, 0, 0)),
                   pl.BlockSpec((1, LANES, t * LANES), lambda i: (i, 0, 0)),
                   grp],
        out_shape=[jax.ShapeDtypeStruct((tiles, t * LANES, 2 * LANES), BF16),
                   jax.ShapeDtypeStruct((tiles, LANES, t * LANES), BF16),
                   jax.ShapeDtypeStruct((g, LANES), F32)],
        compiler_params=_cparams(("parallel",)),
        name="ssm_prep",
    )(*args)
    a_t = pt.reshape(tiles, GROUPS_PER_TILE, 2, p).transpose(0, 2, 1, 3).reshape(tiles, 1, 2 * STATE_LANES)
    zero = jnp.zeros_like(cr)
    c_sel = jnp.concatenate([cr, zero, -ci, zero], axis=-1).astype(BF16)
    return q, cat, c_sel, a_t


def _gelu_tanh(x):
    return 0.5 * x * (1.0 + jnp.tanh(math.sqrt(2.0 / math.pi) * (x + 0.044715 * x * x * x)))


SSM_POS_PER_DOT = 4


def _ssm_kernel(x_ref, q_ref, cat_ref, ca_ref, at_ref, h0_ref, d_ref, y_ref, hf_ref,
                xs_ref, ys_ref, g_ref, hp_ref, bexp_ref, cexp_ref, yi_ref, *, n_prompt_chunks):
    t = SSM_CHUNK
    nq = SSM_POS_PER_DOT
    qw = nq * LANES
    nch = x_ref.shape[0] // t
    ns = nch - n_prompt_chunks
    sl = STATE_LANES
    ii = lambda shape, dim: lax.broadcasted_iota(jnp.int32, shape, dim)

    half = ii((qw, LANES), 1) // SSM_STATE
    row_g = (ii((qw, LANES), 0) % LANES) // SSM_GROUP
    for b in range(t // nq):
        rows = slice(b * qw, (b + 1) * qw)
        for lt in range(2 * sl // LANES):
            r, g0 = divmod(lt * LANES // SSM_STATE, GROUPS_PER_TILE)
            src = q_ref[0, rows, r * LANES:(r + 1) * LANES]
            bexp_ref[rows, lt * LANES:(lt + 1) * LANES] = jnp.where(row_g == g0 + half, src, jnp.zeros_like(src))
    lane_g = (ii((SSM_STATE, t * LANES), 1) % LANES) // SSM_GROUP
    for rb in range(2 * sl // SSM_STATE):
        r, g0 = divmod(rb, GROUPS_PER_TILE)
        src = cat_ref[0, r * SSM_STATE:(r + 1) * SSM_STATE, :]
        cexp_ref[rb * SSM_STATE:(rb + 1) * SSM_STATE, :] = jnp.where(lane_g == g0, src, jnp.zeros_like(src))
    kr = lax.dot_general(q_ref[0], ca_ref[...], (((1,), (1,)), ((), ())), preferred_element_type=F32)
    same = (ii((LANES, LANES), 0) // SSM_GROUP) == (ii((LANES, LANES), 1) // SSM_GROUP)
    krev = [jnp.where(same, kr[j * LANES:(j + 1) * LANES, :], 0.0).astype(BF16) for j in range(t)]

    xs_ref[...] = x_ref[...].astype(F32)
    us = [xs_ref[pl.ds(j, nch, stride=t), :] for j in range(t)]
    un = jnp.concatenate([u.astype(BF16) for u in us], axis=1)
    g_ref[...] = jnp.dot(un, bexp_ref[...], preferred_element_type=F32)

    zblk = jnp.zeros((LANES, LANES), BF16)
    for b in range(t // nq):
        j_hi = (b + 1) * nq
        cols = []
        for j in range(b * nq, j_hi):
            cols.append(jnp.concatenate(krev[t - 1 - j:] + [zblk] * (j_hi - 1 - j), axis=0))
        kq = jnp.concatenate(cols, axis=1)
        yi_ref[:, b * qw:(b + 1) * qw] = jnp.dot(un[:, :j_hi * LANES], kq, preferred_element_type=F32)

    at = at_ref[0]
    a_re, a_im = at[:, :sl], at[:, sl:]
    rid = ii((SUBLANES, sl), 0)
    pw = [(a_re, a_im)]
    for _ in range(SUBLANES - 1):
        pw.append(_cmul(pw[-1][0], pw[-1][1], a_re, a_im))
    car_re = jnp.concatenate([p[0] for p in pw], axis=0)
    car_im = jnp.concatenate([p[1] for p in pw], axis=0)
    levels = []
    s = 1
    while s < SUBLANES:
        levels.append((s, jnp.where(rid >= s, pw[s - 1][0], 0.0), jnp.where(rid >= s, pw[s - 1][1], 0.0)))
        s *= 2

    def block(b, carry):
        c_re, c_im = carry
        r0 = b * SUBLANES
        g = g_ref[pl.ds(r0, SUBLANES), :]
        h_re, h_im = g[:, :sl], g[:, sl:]
        for s, l_re, l_im in levels:
            s_re, s_im = pltpu.roll(h_re, s, 0), pltpu.roll(h_im, s, 0)
            d_re, d_im = _cmul(l_re, l_im, s_re, s_im)
            h_re, h_im = h_re + d_re, h_im + d_im
        d_re, d_im = _cmul(car_re, car_im, c_re, c_im)
        h_re, h_im = h_re + d_re, h_im + d_im
        p_re = jnp.where(rid == 0, c_re, pltpu.roll(h_re, 1, 0))
        p_im = jnp.where(rid == 0, c_im, pltpu.roll(h_im, 1, 0))
        hp_ref[pl.ds(r0, SUBLANES), :] = jnp.concatenate([p_re, p_im], axis=1)
        return h_re[SUBLANES - 1:, :], h_im[SUBLANES - 1:, :]

    e_re = e_im = jnp.zeros((1, sl), F32)
    for b in range(n_prompt_chunks // SUBLANES):
        e_re, e_im = block(b, (e_re, e_im))

    h0 = h0_ref[0]
    hp_ref[n_prompt_chunks:, :] = h0
    gs = g_ref[n_prompt_chunks:, :]
    s_re, s_im = _cmul(a_re, a_im, h0[:, :sl], h0[:, sl:])
    hf_ref[0, 0:ns, :] = jnp.concatenate([s_re + gs[:, :sl], s_im + gs[:, sl:]], axis=1)
    hf_ref[0, ns:, :] = jnp.broadcast_to(jnp.concatenate([e_re, e_im], axis=1), (hf_ref.shape[1] - ns, 2 * sl))

    hp = hp_ref[...].astype(BF16)
    dsk = d_ref[...]
    for b in range(t // nq):
        y = yi_ref[:, b * qw:(b + 1) * qw]
        y = y + jnp.dot(hp, cexp_ref[:, b * qw:(b + 1) * qw], preferred_element_type=F32)
        for jj in range(nq):
            j = b * nq + jj
            ys_ref[pl.ds(j, nch, stride=t), :] = _gelu_tanh(y[:, jj * LANES:(jj + 1) * LANES] + dsk * us[j])
    y_ref[...] = ys_ref[...].astype(y_ref.dtype)


def _ssm(xn, tables, h0, d_skip, n_prompt_chunks):
    q, cat, c_sel, a_t = tables
    m, d = xn.shape
    tiles = d // LANES
    t = SSM_CHUNK
    nch = m // t
    ns = nch - n_prompt_chunks
    assert n_prompt_chunks % SUBLANES == 0 and ns % SUBLANES == 0 and t % SSM_POS_PER_DOT == 0
    hf_rows = 2 * ns
    tl = t * LANES
    s2 = 2 * STATE_LANES
    return pl.pallas_call(
        functools.partial(_ssm_kernel, n_prompt_chunks=n_prompt_chunks),
        grid=(tiles,),
        in_specs=[pl.BlockSpec((m, LANES), lambda i: (0, i)),
                  pl.BlockSpec((1, tl, 2 * LANES), lambda i: (i, 0, 0)),
                  pl.BlockSpec((1, LANES, tl), lambda i: (i, 0, 0)),
                  pl.BlockSpec((LANES, 2 * LANES), lambda i: (i, 0)),
                  pl.BlockSpec((1, 1, s2), lambda i: (i, 0, 0)),
                  pl.BlockSpec((1, ns, s2), lambda i: (i, 0, 0)),
                  pl.BlockSpec((1, LANES), lambda i: (0, i))],
        out_specs=[pl.BlockSpec((m, LANES), lambda i: (0, i)),
                   pl.BlockSpec((1, hf_rows, s2), lambda i: (i, 0, 0))],
        out_shape=[jax.ShapeDtypeStruct((m, d), BF16),
                   jax.ShapeDtypeStruct((tiles, hf_rows, s2), F32)],
        scratch_shapes=[pltpu.VMEM((m, LANES), F32),
                        pltpu.VMEM((m, LANES), F32),
                        pltpu.VMEM((nch, s2), F32),
                        pltpu.VMEM((nch, s2), F32),
                        pltpu.VMEM((tl, s2), BF16),
                        pltpu.VMEM((s2, tl), BF16),
                        pltpu.VMEM((nch, tl), F32)],
        compiler_params=_cparams(("parallel",)),
        name="ssm",
    )(xn, q, cat, c_sel, a_t, h0, d_skip.reshape(1, d))


def _state_to_tiles(h_re, h_im):
    b, g, p = h_re.shape
    tiles = g // GROUPS_PER_TILE
    re = h_re.reshape(b, tiles, STATE_LANES).transpose(1, 0, 2)
    im = h_im.reshape(b, tiles, STATE_LANES).transpose(1, 0, 2)
    return jnp.concatenate([re, im], axis=-1)


def _tiles_to_state(h):
    tiles, b, _ = h.shape
    re = h[..., :STATE_LANES].transpose(1, 0, 2).reshape(b, tiles * GROUPS_PER_TILE, SSM_STATE)
    im = h[..., STATE_LANES:].transpose(1, 0, 2).reshape(b, tiles * GROUPS_PER_TILE, SSM_STATE)
    return re, im


UP_ROWS, UP_COLS = 2080, 512
DOWN_ROWS, DOWN_COLS, DOWN_DEPTH = 1664, 1024, 2048
GLU_ROWS, GLU_COLS = 2080, 256
ADD_NORM_TILE, NORM_TILE = 640, 832
POOL_TILE = 256


def kernel(x_prompt, x_sample, cache_pool, state_ssm_re, state_ssm_im, norm_mix, w_pool, pool_scale, ssm_a_re, ssm_a_im, ssm_log_dt, ssm_b_re, ssm_b_im, ssm_c_re, ssm_c_im, ssm_d, w_glu_a, w_glu_b, norm_ffn, w_up, w_down, norm_final):
    bp, lp, d = x_prompt.shape
    bs, ls, _ = x_sample.shape
    assert bp == 1 and ls == SSM_CHUNK and ls == HIST_PAD and lp % SSM_CHUNK == 0
    ms = bs * ls

    hist_s = jnp.pad(cache_pool[0], ((0, 0), (1, 0), (0, 0))).reshape(ms, d)
    x1, xn, hist_p, xn_s = _pool_layer(x_prompt[0], x_sample.reshape(ms, d), hist_s, norm_mix[0],
                                       w_pool[0], pool_scale[0], norm_ffn[0], POOL_TILE, ls)
    pool_rows_p = hist_p[1:].reshape(1, 1, POOL_HIST, d)
    pool_rows_s = xn_s.reshape(bs, ls, d)[:, 1:].reshape(1, bs, POOL_HIST, d)

    h = _mlp_up(xn, w_up, 0, UP_ROWS, UP_COLS)
    dx1 = _mlp_down(h, w_down, 0, DOWN_ROWS, DOWN_COLS, DOWN_DEPTH)

    xn = _add_norm(x1, dx1, norm_mix[1], BF16, ADD_NORM_TILE)
    tables = _ssm_prep(ssm_a_re[0], ssm_a_im[0], ssm_log_dt[0], ssm_b_re[0], ssm_b_im[0],
                       ssm_c_re[0], ssm_c_im[0])
    h0 = _state_to_tiles(state_ssm_re[0], state_ssm_im[0])
    y, hf = _ssm(xn, tables, h0, ssm_d[0], lp // SSM_CHUNK)
    x3 = _glu(y, w_glu_a, w_glu_b, 0, x1, dx1, GLU_ROWS, GLU_COLS)
    re_s, im_s = _tiles_to_state(hf[:, :bs])
    re_p, im_p = _tiles_to_state(hf[:, bs:bs + 1])

    xn = _rmsnorm(x3, norm_ffn[1], BF16, NORM_TILE)
    h = _mlp_up(xn, w_up, 1, UP_ROWS, UP_COLS)
    dx3 = _mlp_down(h, w_down, 1, DOWN_ROWS, DOWN_COLS, DOWN_DEPTH)
    y_p = _add_norm(x3, dx3, norm_final, F32, 512, 0, lp).reshape(1, lp, d)
    y_s = _add_norm(x3, dx3, norm_final, F32, ms, lp, ms).reshape(bs, ls, d)
    return (y_p, y_s, pool_rows_p, pool_rows_s, re_p[None], im_p[None], re_s[None], im_s[None])
```

```python
import functools
import math

import jax
import jax.numpy as jnp
from jax import lax
from jax.experimental import pallas as pl
from jax.experimental.pallas import tpu as pltpu

F32 = jnp.float32
BF16 = jnp.bfloat16

EPS = 1e-6
PAST_LEN = 1024
POOL_WINDOWS = (2, 4, 8, 16)
POOL_HIST = max(POOL_WINDOWS) - 1
HIST_PAD = POOL_HIST + 1
SSM_GROUP = 16
SSM_STATE = 64
SSM_CHUNK = 16
LANES = 128
SUBLANES = 8
GROUPS_PER_TILE = LANES // SSM_GROUP
STATE_LANES = GROUPS_PER_TILE * SSM_STATE
VMEM_LIMIT = 56 * 1024 * 1024
VMEM_LIMIT_UP = 60 * 1024 * 1024
VMEM_LIMIT_DOWN = 62 * 1024 * 1024


def _cparams(sem, vmem_limit=VMEM_LIMIT):
    return pltpu.CompilerParams(dimension_semantics=sem, vmem_limit_bytes=vmem_limit)


def _rms(x, g):
    return x * lax.rsqrt(jnp.mean(x * x, axis=-1, keepdims=True) + EPS) * g


def _rmsnorm_kernel(x_ref, g_ref, o_ref):
    o_ref[...] = _rms(x_ref[...], g_ref[...]).astype(o_ref.dtype)


def _rmsnorm(x, g, out_dtype, tm):
    m, d = x.shape
    return pl.pallas_call(
        _rmsnorm_kernel,
        grid=(m // tm,),
        in_specs=[pl.BlockSpec((tm, d), lambda i: (i, 0)),
                  pl.BlockSpec((1, d), lambda i: (0, 0))],
        out_specs=pl.BlockSpec((tm, d), lambda i: (i, 0)),
        out_shape=jax.ShapeDtypeStruct((m, d), out_dtype),
        compiler_params=_cparams(("parallel",)),
        name="rmsnorm",
    )(x, g.reshape(1, d))


def _add_norm_kernel(x_ref, dx_ref, g_ref, o_ref):
    o_ref[...] = _rms(x_ref[...] + dx_ref[...].astype(F32), g_ref[...]).astype(o_ref.dtype)


def _add_norm(x, dx, g, out_dtype, tm, row0=0, rows=None):
    m, d = x.shape
    rows = m - row0 if rows is None else rows
    assert rows % tm == 0 and row0 % tm == 0
    off = row0 // tm
    src = pl.BlockSpec((tm, d), lambda i: (i + off, 0))
    return pl.pallas_call(
        _add_norm_kernel,
        grid=(rows // tm,),
        in_specs=[src, src, pl.BlockSpec((1, d), lambda i: (0, 0))],
        out_specs=pl.BlockSpec((tm, d), lambda i: (i, 0)),
        out_shape=jax.ShapeDtypeStruct((rows, d), out_dtype),
        compiler_params=_cparams(("parallel",)),
        name="add_norm",
    )(x, dx, g.reshape(1, d))


def _pool_finish(x, a, s_scaled, gi, pg, w_ref, sc_ref, x1_ref, rows):
    sl = slice(gi * pg, (gi + 1) * pg)
    pooled = s_scaled - a
    mixed = jnp.dot(pooled.astype(BF16), w_ref[gi], preferred_element_type=F32)
    x1 = x[:, sl] + mixed * sc_ref[:, sl]
    x1_ref[0:rows, sl] = x1
    return jnp.sum(x1 * x1, axis=-1, keepdims=True)


def _pool_kernel(xp_ref, xs_ref, hs_ref, gm_ref, wf_ref, sc_ref, gf_ref,
                 x1_ref, xnf_ref, hist_ref, xns_ref, ext_ref, w_ref, *, tm, n_prompt_tiles, seq):
    i = pl.program_id(0)
    d = xp_ref.shape[1]
    pg = d // len(POOL_WINDOWS)

    @pl.when(i == 0)
    def _():
        ext_ref[0:HIST_PAD, :] = jnp.zeros((HIST_PAD, d), F32)
        for gi in range(len(POOL_WINDOWS)):
            w_ref[gi] = wf_ref[gi].astype(BF16)

    @pl.when(i < n_prompt_tiles)
    def _():
        x = xp_ref[...]
        ext_ref[HIST_PAD:, :] = _rms(x, gm_ref[...])
        pos = i * tm + lax.broadcasted_iota(jnp.int32, (tm, 1), 0)
        ss = jnp.zeros((tm, 1), F32)
        for gi, w in enumerate(POOL_WINDOWS):
            a = ext_ref[:, gi * pg:(gi + 1) * pg]
            s = a
            span = 1
            while span < w:
                s = s + pltpu.roll(s, span, 0)
                span *= 2
            inv_cnt = 1.0 / jnp.minimum(w, pos + 1).astype(F32)
            ss = ss + _pool_finish(x, a[HIST_PAD:, :], s[HIST_PAD:, :] * inv_cnt, gi, pg, w_ref, sc_ref, x1_ref, tm)
        inv = lax.rsqrt(ss / d + EPS)
        xnf_ref[...] = (x1_ref[...] * inv * gf_ref[...]).astype(xnf_ref.dtype)
        hist_ref[...] = ext_ref[tm:, :]
        ext_ref[0:HIST_PAD, :] = ext_ref[tm:, :]

    @pl.when(i == n_prompt_tiles)
    def _():
        ms = xs_ref.shape[0]
        x = xs_ref[...]
        xns_ref[...] = _rms(x, gm_ref[...])
        t_in_seq = lax.broadcasted_iota(jnp.int32, (ms, 1), 0) % seq
        ss = jnp.zeros((ms, 1), F32)
        for gi, w in enumerate(POOL_WINDOWS):
            sl = slice(gi * pg, (gi + 1) * pg)
            a = xns_ref[:, sl]
            h = hs_ref[:, sl]
            s = a
            for lag in range(1, w):
                cur = pltpu.roll(a, lag, 0)
                old = pltpu.roll(h, (ms - HIST_PAD + lag) % ms, 0)
                s = s + jnp.where(t_in_seq >= lag, cur, old)
            ss = ss + _pool_finish(x, a, s / float(w), gi, pg, w_ref, sc_ref, x1_ref, ms)
        inv = lax.rsqrt(ss / d + EPS)
        xnf_ref[0:ms, :] = (x1_ref[0:ms, :] * inv * gf_ref[...]).astype(xnf_ref.dtype)


def _pool_layer(x_prompt, x_sample, hist_sample, g_mix, w_pool, scale, g_ffn, tm, seq):
    l, d = x_prompt.shape
    ms = x_sample.shape[0]
    assert l % tm == 0 and ms <= tm and PAST_LEN >= POOL_HIST
    nt = l // tm
    ng, pg, _ = w_pool.shape
    vec = pl.BlockSpec((1, d), lambda i: (0, 0))
    whole = pl.BlockSpec((ms, d), lambda i: (0, 0))
    tile = pl.BlockSpec((tm, d), lambda i: (i, 0))
    return pl.pallas_call(
        functools.partial(_pool_kernel, tm=tm, n_prompt_tiles=nt, seq=seq),
        grid=(nt + 1,),
        in_specs=[pl.BlockSpec((tm, d), lambda i: (jnp.minimum(i, nt - 1), 0)), whole, whole, vec,
                  pl.BlockSpec((ng, pg, pg), lambda i: (0, 0, 0), pipeline_mode=pl.Buffered(1)), vec, vec],
        out_specs=[tile, tile, pl.BlockSpec((HIST_PAD, d), lambda i: (0, 0)), whole],
        out_shape=[jax.ShapeDtypeStruct((l + ms, d), F32),
                   jax.ShapeDtypeStruct((l + ms, d), BF16),
                   jax.ShapeDtypeStruct((HIST_PAD, d), F32),
                   jax.ShapeDtypeStruct((ms, d), F32)],
        scratch_shapes=[pltpu.VMEM((HIST_PAD + tm, d), F32), pltpu.VMEM((ng, pg, pg), BF16)],
        compiler_params=_cparams(("arbitrary",)),
        name="pool_layer",
    )(x_prompt, x_sample, hist_sample, g_mix.reshape(1, d), w_pool, scale.reshape(1, d), g_ffn.reshape(1, d))


def _up_kernel(x_ref, w_ref, o_ref):
    acc = jnp.dot(x_ref[...], w_ref[...].astype(BF16), preferred_element_type=F32)
    h = jnp.maximum(acc, 0.0)
    o_ref[...] = (h * h).astype(o_ref.dtype)


def _mlp_up(xn, w, layer, tm, tn):
    m, k = xn.shape
    n = w.shape[2]
    return pl.pallas_call(
        _up_kernel,
        grid=(m // tm, n // tn),
        in_specs=[pl.BlockSpec((tm, k), lambda i, j: (i, 0)),
                  pl.BlockSpec((None, k, tn), lambda i, j: (layer, 0, j))],
        out_specs=pl.BlockSpec((tm, tn), lambda i, j: (i, j)),
        out_shape=jax.ShapeDtypeStruct((m, n), BF16),
        compiler_params=_cparams(("parallel", "arbitrary"), VMEM_LIMIT_UP),
        name="mlp_up",
    )(xn, w)


def _down_kernel(h_ref, w_ref, o_ref, acc_ref):
    kk = pl.program_id(2)

    @pl.when(kk == 0)
    def _():
        acc_ref[...] = jnp.zeros_like(acc_ref)

    acc_ref[...] += jnp.dot(h_ref[...], w_ref[...].astype(BF16), preferred_element_type=F32)

    @pl.when(kk == pl.num_programs(2) - 1)
    def _():
        o_ref[...] = acc_ref[...].astype(o_ref.dtype)


def _mlp_down(h, w, layer, tm, tn, tk):
    m, k = h.shape
    n = w.shape[2]
    return pl.pallas_call(
        _down_kernel,
        grid=(m // tm, n // tn, k // tk),
        in_specs=[pl.BlockSpec((tm, tk), lambda i, j, kk: (i, kk)),
                  pl.BlockSpec((None, tk, tn), lambda i, j, kk: (layer, kk, j))],
        out_specs=pl.BlockSpec((tm, tn), lambda i, j, kk: (i, j)),
        out_shape=jax.ShapeDtypeStruct((m, n), BF16),
        scratch_shapes=[pltpu.VMEM((tm, tn), F32)],
        compiler_params=_cparams(("parallel", "parallel", "arbitrary"), VMEM_LIMIT_DOWN),
        name="mlp_down",
    )(h, w)


def _glu_kernel(y_ref, wa_ref, wb_ref, r_ref, dr_ref, o_ref):
    y = y_ref[...]
    a = jnp.dot(y, wa_ref[...].astype(BF16), preferred_element_type=F32)
    b = jnp.dot(y, wb_ref[...].astype(BF16), preferred_element_type=F32)
    o_ref[...] = (r_ref[...] + dr_ref[...].astype(F32)) + a * jax.nn.sigmoid(b)


def _glu(y, wa, wb, layer, resid, dresid, tm, tn):
    m, k = y.shape
    n = wa.shape[2]
    wspec = pl.BlockSpec((None, k, tn), lambda i, j: (layer, 0, j))
    tile = pl.BlockSpec((tm, tn), lambda i, j: (i, j))
    return pl.pallas_call(
        _glu_kernel,
        grid=(m // tm, n // tn),
        in_specs=[pl.BlockSpec((tm, k), lambda i, j: (i, 0), pipeline_mode=pl.Buffered(1)),
                  wspec, wspec, tile, tile],
        out_specs=tile,
        out_shape=jax.ShapeDtypeStruct((m, n), F32),
        compiler_params=_cparams(("parallel", "arbitrary")),
        name="glu",
    )(y, wa, wb, resid, dresid)


def _cmul(a_re, a_im, b_re, b_im):
    return a_re * b_re - a_im * b_im, a_re * b_im + a_im * b_re


def _ssm_prep_kernel(are_ref, aim_ref, ldt_ref, brr_ref, bii_ref, ca_ref, cb_ref, q_ref, cat_ref, pt_ref):
    t = SSM_CHUNK
    rows = brr_ref.shape[0]

    def per_channel(v):
        return jnp.concatenate([jnp.broadcast_to(v[g:g + 1, :], (SSM_GROUP, v.shape[1])) for g in range(v.shape[0])],
                               axis=0)

    l_re, l_im = are_ref[...], aim_ref[...]
    dt = jnp.exp(ldt_ref[...])
    mag = jnp.exp(l_re * dt)
    a_re, a_im = mag * jnp.cos(l_im * dt), mag * jnp.sin(l_im * dt)
    den = l_re * l_re + l_im * l_im
    n_re, n_im = a_re - 1.0, a_im
    k_re = (n_re * l_re + n_im * l_im) / den
    k_im = (n_im * l_re - n_re * l_im) / den
    bb_re, bb_im = _cmul(per_channel(k_re), per_channel(k_im), brr_ref[...], bii_ref[...])
    c_a, c_b = ca_ref[...], cb_ref[...]
    p_re, p_im = jnp.ones_like(a_re), jnp.zeros_like(a_re)
    for n in range(t):
        j = t - 1 - n
        q_re, q_im = _cmul(per_channel(p_re), per_channel(p_im), bb_re, bb_im)
        q_ref[0, j * rows:(j + 1) * rows, 0:LANES] = q_re.astype(q_ref.dtype)
        q_ref[0, j * rows:(j + 1) * rows, LANES:] = q_im.astype(q_ref.dtype)
        p_re, p_im = _cmul(p_re, p_im, a_re, a_im)
        cat = per_channel(p_re) * c_a + per_channel(p_im) * c_b
        cat_ref[0, :, n * rows:(n + 1) * rows] = cat.T.astype(cat_ref.dtype)
    half = lax.broadcasted_iota(jnp.int32, p_re.shape, 1) < (p_re.shape[1] // 2)
    pt_ref[...] = jnp.where(half, p_re, p_im)


def _ssm_prep(a_re, a_im, log_dt, b_re, b_im, c_re, c_im):
    g, p = a_re.shape
    c = b_re.shape[-1]
    t = SSM_CHUNK
    tiles = g // GROUPS_PER_TILE
    rows = g * c
    assert 2 * p == LANES and c == SSM_GROUP
    dup = lambda v: jnp.concatenate([v, v], axis=-1)
    bt_re = b_re.transpose(0, 2, 1).reshape(rows, p)
    bt_im = b_im.transpose(0, 2, 1).reshape(rows, p)
    cr, ci = c_re.reshape(rows, p), c_im.reshape(rows, p)
    args = (dup(a_re), dup(a_im), dup(jnp.broadcast_to(log_dt[:, None], (g, p))),
            dup(bt_re), dup(bt_im),
            jnp.concatenate([cr, -ci], axis=-1), jnp.concatenate([-ci, -cr], axis=-1))
    grp = pl.BlockSpec((GROUPS_PER_TILE, LANES), lambda i: (i, 0))
    blk = pl.BlockSpec((LANES, LANES), lambda i: (i, 0))
    q, cat, pt = pl.pallas_call(
        _ssm_prep_kernel,
        grid=(tiles,),
        in_specs=[grp] * 3 + [blk] * 4,
        out_specs=[pl.BlockSpec((1, t * LANES, 2 * LANES), lambda i: (i, 0, 0)),
                   pl.BlockSpec((1, LANES, t * LANES), lambda i: (i, 0, 0)),
                   grp],
        out_shape=[jax.ShapeDtypeStruct((tiles, t * LANES, 2 * LANES), BF16),
                   jax.ShapeDtypeStruct((tiles, LANES, t * LANES), BF16),
                   jax.ShapeDtypeStruct((g, LANES), F32)],
        compiler_params=_cparams(("parallel",)),
        name="ssm_prep",
    )(*args)
    a_t = pt.reshape(tiles, GROUPS_PER_TILE, 2, p).transpose(0, 2, 1, 3).reshape(tiles, 1, 2 * STATE_LANES)
    zero = jnp.zeros_like(cr)
    c_sel = jnp.concatenate([cr, zero, -ci, zero], axis=-1).astype(BF16)
    return q, cat, c_sel, a_t


def _gelu_tanh(x):
    return 0.5 * x * (1.0 + jnp.tanh(math.sqrt(2.0 / math.pi) * (x + 0.044715 * x * x * x)))


SSM_POS_PER_DOT = 4


def _ssm_kernel(x_ref, q_ref, cat_ref, ca_ref, at_ref, h0_ref, d_ref, y_ref, hf_ref,
                xs_ref, ys_ref, g_ref, hp_ref, bexp_ref, cexp_ref, yi_ref, *, n_prompt_chunks):
    t = SSM_CHUNK
    nq = SSM_POS_PER_DOT
    qw = nq * LANES
    nch = x_ref.shape[0] // t
    ns = nch - n_prompt_chunks
    sl = STATE_LANES
    ii = lambda shape, dim: lax.broadcasted_iota(jnp.int32, shape, dim)

    half = ii((qw, LANES), 1) // SSM_STATE
    row_g = (ii((qw, LANES), 0) % LANES) // SSM_GROUP
    for b in range(t // nq):
        rows = slice(b * qw, (b + 1) * qw)
        for lt in range(2 * sl // LANES):
            r, g0 = divmod(lt * LANES // SSM_STATE, GROUPS_PER_TILE)
            src = q_ref[0, rows, r * LANES:(r + 1) * LANES]
            bexp_ref[rows, lt * LANES:(lt + 1) * LANES] = jnp.where(row_g == g0 + half, src, jnp.zeros_like(src))
    lane_g = (ii((SSM_STATE, t * LANES), 1) % LANES) // SSM_GROUP
    for rb in range(2 * sl // SSM_STATE):
        r, g0 = divmod(rb, GROUPS_PER_TILE)
        src = cat_ref[0, r * SSM_STATE:(r + 1) * SSM_STATE, :]
        cexp_ref[rb * SSM_STATE:(rb + 1) * SSM_STATE, :] = jnp.where(lane_g == g0, src, jnp.zeros_like(src))
    kr = lax.dot_general(q_ref[0], ca_ref[...], (((1,), (1,)), ((), ())), preferred_element_type=F32)
    same = (ii((LANES, LANES), 0) // SSM_GROUP) == (ii((LANES, LANES), 1) // SSM_GROUP)
    krev = [jnp.where(same, kr[j * LANES:(j + 1) * LANES, :], 0.0).astype(BF16) for j in range(t)]

    xs_ref[...] = x_ref[...].astype(F32)
    us = [xs_ref[pl.ds(j, nch, stride=t), :] for j in range(t)]
    un = jnp.concatenate([u.astype(BF16) for u in us], axis=1)
    g_ref[...] = jnp.dot(un, bexp_ref[...], preferred_element_type=F32)

    zblk = jnp.zeros((LANES, LANES), BF16)
    for b in range(t // nq):
        j_hi = (b + 1) * nq
        cols = []
        for j in range(b * nq, j_hi):
            cols.append(jnp.concatenate(krev[t - 1 - j:] + [zblk] * (j_hi - 1 - j), axis=0))
        kq = jnp.concatenate(cols, axis=1)
        yi_ref[:, b * qw:(b + 1) * qw] = jnp.dot(un[:, :j_hi * LANES], kq, preferred_element_type=F32)

    at = at_ref[0]
    a_re, a_im = at[:, :sl], at[:, sl:]
    rid = ii((SUBLANES, sl), 0)
    pw = [(a_re, a_im)]
    for _ in range(SUBLANES - 1):
        pw.append(_cmul(pw[-1][0], pw[-1][1], a_re, a_im))
    car_re = jnp.concatenate([p[0] for p in pw], axis=0)
    car_im = jnp.concatenate([p[1] for p in pw], axis=0)
    levels = []
    s = 1
    while s < SUBLANES:
        levels.append((s, jnp.where(rid >= s, pw[s - 1][0], 0.0), jnp.where(rid >= s, pw[s - 1][1], 0.0)))
        s *= 2

    def block(b, carry):
        c_re, c_im = carry
        r0 = b * SUBLANES
        g = g_ref[pl.ds(r0, SUBLANES), :]
        h_re, h_im = g[:, :sl], g[:, sl:]
        for s, l_re, l_im in levels:
            s_re, s_im = pltpu.roll(h_re, s, 0), pltpu.roll(h_im, s, 0)
            d_re, d_im = _cmul(l_re, l_im, s_re, s_im)
            h_re, h_im = h_re + d_re, h_im + d_im
        d_re, d_im = _cmul(car_re, car_im, c_re, c_im)
        h_re, h_im = h_re + d_re, h_im + d_im
        p_re = jnp.where(rid == 0, c_re, pltpu.roll(h_re, 1, 0))
        p_im = jnp.where(rid == 0, c_im, pltpu.roll(h_im, 1, 0))
        hp_ref[pl.ds(r0, SUBLANES), :] = jnp.concatenate([p_re, p_im], axis=1)
        return h_re[SUBLANES - 1:, :], h_im[SUBLANES - 1:, :]

    e_re = e_im = jnp.zeros((1, sl), F32)
    for b in range(n_prompt_chunks // SUBLANES):
        e_re, e_im = block(b, (e_re, e_im))

    h0 = h0_ref[0]
    hp_ref[n_prompt_chunks:, :] = h0
    gs = g_ref[n_prompt_chunks:, :]
    s_re, s_im = _cmul(a_re, a_im, h0[:, :sl], h0[:, sl:])
    hf_ref[0, 0:ns, :] = jnp.concatenate([s_re + gs[:, :sl], s_im + gs[:, sl:]], axis=1)
    hf_ref[0, ns:, :] = jnp.broadcast_to(jnp.concatenate([e_re, e_im], axis=1), (hf_ref.shape[1] - ns, 2 * sl))

    hp = hp_ref[...].astype(BF16)
    dsk = d_ref[...]
    for b in range(t // nq):
        y = yi_ref[:, b * qw:(b + 1) * qw]
        y = y + jnp.dot(hp, cexp_ref[:, b * qw:(b + 1) * qw], preferred_element_type=F32)
        for jj in range(nq):
            j = b * nq + jj
            ys_ref[pl.ds(j, nch, stride=t), :] = _gelu_tanh(y[:, jj * LANES:(jj + 1) * LANES] + dsk * us[j])
    y_ref[...] = ys_ref[...].astype(y_ref.dtype)


def _ssm(xn, tables, h0, d_skip, n_prompt_chunks):
    q, cat, c_sel, a_t = tables
    m, d = xn.shape
    tiles = d // LANES
    t = SSM_CHUNK
    nch = m // t
    ns = nch - n_prompt_chunks
    assert n_prompt_chunks % SUBLANES == 0 and ns % SUBLANES == 0 and t % SSM_POS_PER_DOT == 0
    hf_rows = 2 * ns
    tl = t * LANES
    s2 = 2 * STATE_LANES
    return pl.pallas_call(
        functools.partial(_ssm_kernel, n_prompt_chunks=n_prompt_chunks),
        grid=(tiles,),
        in_specs=[pl.BlockSpec((m, LANES), lambda i: (0, i)),
                  pl.BlockSpec((1, tl, 2 * LANES), lambda i: (i, 0, 0)),
                  pl.BlockSpec((1, LANES, tl), lambda i: (i, 0, 0)),
                  pl.BlockSpec((LANES, 2 * LANES), lambda i: (i, 0)),
                  pl.BlockSpec((1, 1, s2), lambda i: (i, 0, 0)),
                  pl.BlockSpec((1, ns, s2), lambda i: (i, 0, 0)),
                  pl.BlockSpec((1, LANES), lambda i: (0, i))],
        out_specs=[pl.BlockSpec((m, LANES), lambda i: (0, i)),
                   pl.BlockSpec((1, hf_rows, s2), lambda i: (i, 0, 0))],
        out_shape=[jax.ShapeDtypeStruct((m, d), BF16),
                   jax.ShapeDtypeStruct((tiles, hf_rows, s2), F32)],
        scratch_shapes=[pltpu.VMEM((m, LANES), F32),
                        pltpu.VMEM((m, LANES), F32),
                        pltpu.VMEM((nch, s2), F32),
                        pltpu.VMEM((nch, s2), F32),
                        pltpu.VMEM((tl, s2), BF16),
                        pltpu.VMEM((s2, tl), BF16),
                        pltpu.VMEM((nch, tl), F32)],
        compiler_params=_cparams(("parallel",)),
        name="ssm",
    )(xn, q, cat, c_sel, a_t, h0, d_skip.reshape(1, d))


def _state_to_tiles(h_re, h_im):
    b, g, p = h_re.shape
    tiles = g // GROUPS_PER_TILE
    re = h_re.reshape(b, tiles, STATE_LANES).transpose(1, 0, 2)
    im = h_im.reshape(b, tiles, STATE_LANES).transpose(1, 0, 2)
    return jnp.concatenate([re, im], axis=-1)


def _tiles_to_state(h):
    tiles, b, _ = h.shape
    re = h[..., :STATE_LANES].transpose(1, 0, 2).reshape(b, tiles * GROUPS_PER_TILE, SSM_STATE)
    im = h[..., STATE_LANES:].transpose(1, 0, 2).reshape(b, tiles * GROUPS_PER_TILE, SSM_STATE)
    return re, im


UP_ROWS, UP_COLS = 2080, 512
DOWN_ROWS, DOWN_COLS, DOWN_DEPTH = 2080, 1024, 2048
GLU_ROWS, GLU_COLS = 2080, 256
ADD_NORM_TILE, NORM_TILE = 640, 832
POOL_TILE = 256


def kernel(x_prompt, x_sample, cache_pool, state_ssm_re, state_ssm_im, norm_mix, w_pool, pool_scale, ssm_a_re, ssm_a_im, ssm_log_dt, ssm_b_re, ssm_b_im, ssm_c_re, ssm_c_im, ssm_d, w_glu_a, w_glu_b, norm_ffn, w_up, w_down, norm_final):
    bp, lp, d = x_prompt.shape
    bs, ls, _ = x_sample.shape
    assert bp == 1 and ls == SSM_CHUNK and ls == HIST_PAD and lp % SSM_CHUNK == 0
    ms = bs * ls

    hist_s = jnp.pad(cache_pool[0], ((0, 0), (1, 0), (0, 0))).reshape(ms, d)
    x1, xn, hist_p, xn_s = _pool_layer(x_prompt[0], x_sample.reshape(ms, d), hist_s, norm_mix[0],
                                       w_pool[0], pool_scale[0], norm_ffn[0], POOL_TILE, ls)
    pool_rows_p = hist_p[1:].reshape(1, 1, POOL_HIST, d)
    pool_rows_s = xn_s.reshape(bs, ls, d)[:, 1:].reshape(1, bs, POOL_HIST, d)

    h = _mlp_up(xn, w_up, 0, UP_ROWS, UP_COLS)
    dx1 = _mlp_down(h, w_down, 0, DOWN_ROWS, DOWN_COLS, DOWN_DEPTH)

    xn = _add_norm(x1, dx1, norm_mix[1], BF16, ADD_NORM_TILE)
    tables = _ssm_prep(ssm_a_re[0], ssm_a_im[0], ssm_log_dt[0], ssm_b_re[0], ssm_b_im[0],
                       ssm_c_re[0], ssm_c_im[0])
    h0 = _state_to_tiles(state_ssm_re[0], state_ssm_im[0])
    y, hf = _ssm(xn, tables, h0, ssm_d[0], lp // SSM_CHUNK)
    x3 = _glu(y, w_glu_a, w_glu_b, 0, x1, dx1, GLU_ROWS, GLU_COLS)
    re_s, im_s = _tiles_to_state(hf[:, :bs])
    re_p, im_p = _tiles_to_state(hf[:, bs:bs + 1])

    xn = _rmsnorm(x3, norm_ffn[1], BF16, NORM_TILE)
    h = _mlp_up(xn, w_up, 1, UP_ROWS, UP_COLS)
    dx3 = _mlp_down(h, w_down, 1, DOWN_ROWS, DOWN_COLS, DOWN_DEPTH)
    y_p = _add_norm(x3, dx3, norm_final, F32, 512, 0, lp).reshape(1, lp, d)
    y_s = _add_norm(x3, dx3, norm_final, F32, ms, lp, ms).reshape(bs, ls, d)
    return (y_p, y_s, pool_rows_p, pool_rows_s, re_p[None], im_p[None], re_s[None], im_s[None])
```

```python
import functools
import math

import jax
import jax.numpy as jnp
from jax import lax
from jax.experimental import pallas as pl
from jax.experimental.pallas import tpu as pltpu

F32 = jnp.float32
BF16 = jnp.bfloat16

EPS = 1e-6
PAST_LEN = 1024
POOL_WINDOWS = (2, 4, 8, 16)
POOL_HIST = max(POOL_WINDOWS) - 1
HIST_PAD = POOL_HIST + 1
SSM_GROUP = 16
SSM_STATE = 64
SSM_CHUNK = 16
LANES = 128
SUBLANES = 8
GROUPS_PER_TILE = LANES // SSM_GROUP
STATE_LANES = GROUPS_PER_TILE * SSM_STATE
VMEM_LIMIT = 56 * 1024 * 1024
VMEM_LIMIT_UP = 60 * 1024 * 1024
VMEM_LIMIT_BIG = 62 * 1024 * 1024


def _cparams(sem, vmem_limit=VMEM_LIMIT):
    return pltpu.CompilerParams(dimension_semantics=sem, vmem_limit_bytes=vmem_limit)


def _rms(x, g):
    return x * lax.rsqrt(jnp.mean(x * x, axis=-1, keepdims=True) + EPS) * g


def _rmsnorm_kernel(x_ref, g_ref, o_ref):
    o_ref[...] = _rms(x_ref[...], g_ref[...]).astype(o_ref.dtype)


def _rmsnorm(x, g, out_dtype, tm):
    m, d = x.shape
    return pl.pallas_call(
        _rmsnorm_kernel,
        grid=(m // tm,),
        in_specs=[pl.BlockSpec((tm, d), lambda i: (i, 0)),
                  pl.BlockSpec((1, d), lambda i: (0, 0))],
        out_specs=pl.BlockSpec((tm, d), lambda i: (i, 0)),
        out_shape=jax.ShapeDtypeStruct((m, d), out_dtype),
        compiler_params=_cparams(("parallel",)),
        name="rmsnorm",
    )(x, g.reshape(1, d))


def _add_norm_kernel(x_ref, dx_ref, g_ref, o_ref):
    o_ref[...] = _rms(x_ref[...] + dx_ref[...].astype(F32), g_ref[...]).astype(o_ref.dtype)


def _add_norm(x, dx, g, out_dtype, tm, row0=0, rows=None):
    m, d = x.shape
    rows = m - row0 if rows is None else rows
    assert rows % tm == 0 and row0 % tm == 0
    off = row0 // tm
    src = pl.BlockSpec((tm, d), lambda i: (i + off, 0))
    return pl.pallas_call(
        _add_norm_kernel,
        grid=(rows // tm,),
        in_specs=[src, src, pl.BlockSpec((1, d), lambda i: (0, 0))],
        out_specs=pl.BlockSpec((tm, d), lambda i: (i, 0)),
        out_shape=jax.ShapeDtypeStruct((rows, d), out_dtype),
        compiler_params=_cparams(("parallel",)),
        name="add_norm",
    )(x, dx, g.reshape(1, d))


def _pool_finish(x, a, s_scaled, gi, pg, w_ref, sc_ref, x1_ref, rows):
    sl = slice(gi * pg, (gi + 1) * pg)
    pooled = s_scaled - a
    mixed = jnp.dot(pooled.astype(BF16), w_ref[gi], preferred_element_type=F32)
    x1 = x[:, sl] + mixed * sc_ref[:, sl]
    x1_ref[0:rows, sl] = x1
    return jnp.sum(x1 * x1, axis=-1, keepdims=True)


def _pool_kernel(xp_ref, xs_ref, hs_ref, gm_ref, wf_ref, sc_ref, gf_ref,
                 x1_ref, xnf_ref, hist_ref, xns_ref, ext_ref, w_ref, *, tm, n_prompt_tiles, seq):
    i = pl.program_id(0)
    d = xp_ref.shape[1]
    pg = d // len(POOL_WINDOWS)

    @pl.when(i == 0)
    def _():
        ext_ref[0:HIST_PAD, :] = jnp.zeros((HIST_PAD, d), F32)
        for gi in range(len(POOL_WINDOWS)):
            w_ref[gi] = wf_ref[gi].astype(BF16)

    @pl.when(i < n_prompt_tiles)
    def _():
        x = xp_ref[...]
        ext_ref[HIST_PAD:, :] = _rms(x, gm_ref[...])
        pos = i * tm + lax.broadcasted_iota(jnp.int32, (tm, 1), 0)
        ss = jnp.zeros((tm, 1), F32)
        for gi, w in enumerate(POOL_WINDOWS):
            a = ext_ref[:, gi * pg:(gi + 1) * pg]
            s = a
            span = 1
            while span < w:
                s = s + pltpu.roll(s, span, 0)
                span *= 2
            inv_cnt = 1.0 / jnp.minimum(w, pos + 1).astype(F32)
            ss = ss + _pool_finish(x, a[HIST_PAD:, :], s[HIST_PAD:, :] * inv_cnt, gi, pg, w_ref, sc_ref, x1_ref, tm)
        inv = lax.rsqrt(ss / d + EPS)
        xnf_ref[...] = (x1_ref[...] * inv * gf_ref[...]).astype(xnf_ref.dtype)
        hist_ref[...] = ext_ref[tm:, :]
        ext_ref[0:HIST_PAD, :] = ext_ref[tm:, :]

    @pl.when(i == n_prompt_tiles)
    def _():
        ms = xs_ref.shape[0]
        x = xs_ref[...]
        xns_ref[...] = _rms(x, gm_ref[...])
        t_in_seq = lax.broadcasted_iota(jnp.int32, (ms, 1), 0) % seq
        ss = jnp.zeros((ms, 1), F32)
        for gi, w in enumerate(POOL_WINDOWS):
            sl = slice(gi * pg, (gi + 1) * pg)
            a = xns_ref[:, sl]
            h = hs_ref[:, sl]
            s = a
            for lag in range(1, w):
                cur = pltpu.roll(a, lag, 0)
                old = pltpu.roll(h, (ms - HIST_PAD + lag) % ms, 0)
                s = s + jnp.where(t_in_seq >= lag, cur, old)
            ss = ss + _pool_finish(x, a, s / float(w), gi, pg, w_ref, sc_ref, x1_ref, ms)
        inv = lax.rsqrt(ss / d + EPS)
        xnf_ref[0:ms, :] = (x1_ref[0:ms, :] * inv * gf_ref[...]).astype(xnf_ref.dtype)


def _pool_layer(x_prompt, x_sample, hist_sample, g_mix, w_pool, scale, g_ffn, tm, seq):
    l, d = x_prompt.shape
    ms = x_sample.shape[0]
    assert l % tm == 0 and ms <= tm and PAST_LEN >= POOL_HIST
    nt = l // tm
    ng, pg, _ = w_pool.shape
    vec = pl.BlockSpec((1, d), lambda i: (0, 0))
    whole = pl.BlockSpec((ms, d), lambda i: (0, 0))
    tile = pl.BlockSpec((tm, d), lambda i: (i, 0))
    return pl.pallas_call(
        functools.partial(_pool_kernel, tm=tm, n_prompt_tiles=nt, seq=seq),
        grid=(nt + 1,),
        in_specs=[pl.BlockSpec((tm, d), lambda i: (jnp.minimum(i, nt - 1), 0)), whole, whole, vec,
                  pl.BlockSpec((ng, pg, pg), lambda i: (0, 0, 0), pipeline_mode=pl.Buffered(1)), vec, vec],
        out_specs=[tile, tile, pl.BlockSpec((HIST_PAD, d), lambda i: (0, 0)), whole],
        out_shape=[jax.ShapeDtypeStruct((l + ms, d), F32),
                   jax.ShapeDtypeStruct((l + ms, d), BF16),
                   jax.ShapeDtypeStruct((HIST_PAD, d), F32),
                   jax.ShapeDtypeStruct((ms, d), F32)],
        scratch_shapes=[pltpu.VMEM((HIST_PAD + tm, d), F32), pltpu.VMEM((ng, pg, pg), BF16)],
        compiler_params=_cparams(("arbitrary",)),
        name="pool_layer",
    )(x_prompt, x_sample, hist_sample, g_mix.reshape(1, d), w_pool, scale.reshape(1, d), g_ffn.reshape(1, d))


def _up_kernel(x_ref, w_ref, o_ref):
    acc = jnp.dot(x_ref[...], w_ref[...].astype(BF16), preferred_element_type=F32)
    h = jnp.maximum(acc, 0.0)
    o_ref[...] = (h * h).astype(o_ref.dtype)


def _mlp_up(xn, w, layer, tm, tn):
    m, k = xn.shape
    n = w.shape[2]
    return pl.pallas_call(
        _up_kernel,
        grid=(m // tm, n // tn),
        in_specs=[pl.BlockSpec((tm, k), lambda i, j: (i, 0)),
                  pl.BlockSpec((None, k, tn), lambda i, j: (layer, 0, j))],
        out_specs=pl.BlockSpec((tm, tn), lambda i, j: (i, j)),
        out_shape=jax.ShapeDtypeStruct((m, n), BF16),
        compiler_params=_cparams(("parallel", "arbitrary"), VMEM_LIMIT_UP),
        name="mlp_up",
    )(xn, w)


def _down_kernel(h_ref, w_ref, o_ref, acc_ref):
    kk = pl.program_id(2)

    @pl.when(kk == 0)
    def _():
        acc_ref[...] = jnp.zeros_like(acc_ref)

    acc_ref[...] += jnp.dot(h_ref[...], w_ref[...].astype(BF16), preferred_element_type=F32)

    @pl.when(kk == pl.num_programs(2) - 1)
    def _():
        o_ref[...] = acc_ref[...].astype(o_ref.dtype)


def _mlp_down(h, w, layer, tm, tn, tk):
    m, k = h.shape
    n = w.shape[2]
    return pl.pallas_call(
        _down_kernel,
        grid=(m // tm, n // tn, k // tk),
        in_specs=[pl.BlockSpec((tm, tk), lambda i, j, kk: (i, kk)),
                  pl.BlockSpec((None, tk, tn), lambda i, j, kk: (layer, kk, j))],
        out_specs=pl.BlockSpec((tm, tn), lambda i, j, kk: (i, j)),
        out_shape=jax.ShapeDtypeStruct((m, n), BF16),
        scratch_shapes=[pltpu.VMEM((tm, tn), F32)],
        compiler_params=_cparams(("parallel", "parallel", "arbitrary"), VMEM_LIMIT_BIG),
        name="mlp_down",
    )(h, w)


def _glu_kernel(y_ref, wa_ref, wb_ref, r_ref, dr_ref, o_ref):
    y = y_ref[...]
    a = jnp.dot(y, wa_ref[...].astype(BF16), preferred_element_type=F32)
    b = jnp.dot(y, wb_ref[...].astype(BF16), preferred_element_type=F32)
    o_ref[...] = (r_ref[...] + dr_ref[...].astype(F32)) + a * jax.nn.sigmoid(b)


def _glu(y, wa, wb, layer, resid, dresid, tm, tn):
    m, k = y.shape
    n = wa.shape[2]
    wspec = pl.BlockSpec((None, k, tn), lambda i, j: (layer, 0, j))
    tile = pl.BlockSpec((tm, tn), lambda i, j: (i, j))
    return pl.pallas_call(
        _glu_kernel,
        grid=(m // tm, n // tn),
        in_specs=[pl.BlockSpec((tm, k), lambda i, j: (i, 0)),
                  wspec, wspec, tile, tile],
        out_specs=tile,
        out_shape=jax.ShapeDtypeStruct((m, n), F32),
        compiler_params=_cparams(("parallel", "arbitrary"), VMEM_LIMIT_BIG),
        name="glu",
    )(y, wa, wb, resid, dresid)


def _cmul(a_re, a_im, b_re, b_im):
    return a_re * b_re - a_im * b_im, a_re * b_im + a_im * b_re


def _ssm_prep_kernel(are_ref, aim_ref, ldt_ref, brr_ref, bii_ref, ca_ref, cb_ref, q_ref, cat_ref, pt_ref):
    t = SSM_CHUNK
    rows = brr_ref.shape[0]

    def per_channel(v):
        return jnp.concatenate([jnp.broadcast_to(v[g:g + 1, :], (SSM_GROUP, v.shape[1])) for g in range(v.shape[0])],
                               axis=0)

    l_re, l_im = are_ref[...], aim_ref[...]
    dt = jnp.exp(ldt_ref[...])
    mag = jnp.exp(l_re * dt)
    a_re, a_im = mag * jnp.cos(l_im * dt), mag * jnp.sin(l_im * dt)
    den = l_re * l_re + l_im * l_im
    n_re, n_im = a_re - 1.0, a_im
    k_re = (n_re * l_re + n_im * l_im) / den
    k_im = (n_im * l_re - n_re * l_im) / den
    bb_re, bb_im = _cmul(per_channel(k_re), per_channel(k_im), brr_ref[...], bii_ref[...])
    c_a, c_b = ca_ref[...], cb_ref[...]
    p_re, p_im = jnp.ones_like(a_re), jnp.zeros_like(a_re)
    for n in range(t):
        j = t - 1 - n
        q_re, q_im = _cmul(per_channel(p_re), per_channel(p_im), bb_re, bb_im)
        q_ref[0, j * rows:(j + 1) * rows, 0:LANES] = q_re.astype(q_ref.dtype)
        q_ref[0, j * rows:(j + 1) * rows, LANES:] = q_im.astype(q_ref.dtype)
        p_re, p_im = _cmul(p_re, p_im, a_re, a_im)
        cat = per_channel(p_re) * c_a + per_channel(p_im) * c_b
        cat_ref[0, :, n * rows:(n + 1) * rows] = cat.T.astype(cat_ref.dtype)
    half = lax.broadcasted_iota(jnp.int32, p_re.shape, 1) < (p_re.shape[1] // 2)
    pt_ref[...] = jnp.where(half, p_re, p_im)


def _ssm_prep(a_re, a_im, log_dt, b_re, b_im, c_re, c_im):
    g, p = a_re.shape
    c = b_re.shape[-1]
    t = SSM_CHUNK
    tiles = g // GROUPS_PER_TILE
    rows = g * c
    assert 2 * p == LANES and c == SSM_GROUP
    dup = lambda v: jnp.concatenate([v, v], axis=-1)
    bt_re = b_re.transpose(0, 2, 1).reshape(rows, p)
    bt_im = b_im.transpose(0, 2, 1).reshape(rows, p)
    cr, ci = c_re.reshape(rows, p), c_im.reshape(rows, p)
    args = (dup(a_re), dup(a_im), dup(jnp.broadcast_to(log_dt[:, None], (g, p))),
            dup(bt_re), dup(bt_im),
            jnp.concatenate([cr, -ci], axis=-1), jnp.concatenate([-ci, -cr], axis=-1))
    grp = pl.BlockSpec((GROUPS_PER_TILE, LANES), lambda i: (i, 0))
    blk = pl.BlockSpec((LANES, LANES), lambda i: (i, 0))
    q, cat, pt = pl.pallas_call(
        _ssm_prep_kernel,
        grid=(tiles,),
        in_specs=[grp] * 3 + [blk] * 4,
        out_specs=[pl.BlockSpec((1, t * LANES, 2 * LANES), lambda i: (i, 0, 0)),
                   pl.BlockSpec((1, LANES, t * LANES), lambda i: (i, 0, 0)),
                   grp],
        out_shape=[jax.ShapeDtypeStruct((tiles, t * LANES, 2 * LANES), BF16),
                   jax.ShapeDtypeStruct((tiles, LANES, t * LANES), BF16),
                   jax.ShapeDtypeStruct((g, LANES), F32)],
        compiler_params=_cparams(("parallel",)),
        name="ssm_prep",
    )(*args)
    a_t = pt.reshape(tiles, GROUPS_PER_TILE, 2, p).transpose(0, 2, 1, 3).reshape(tiles, 1, 2 * STATE_LANES)
    zero = jnp.zeros_like(cr)
    c_sel = jnp.concatenate([cr, zero, -ci, zero], axis=-1).astype(BF16)
    return q, cat, c_sel, a_t


def _gelu_tanh(x):
    return 0.5 * x * (1.0 + jnp.tanh(math.sqrt(2.0 / math.pi) * (x + 0.044715 * x * x * x)))


SSM_POS_PER_DOT = 4


def _ssm_kernel(x_ref, q_ref, cat_ref, ca_ref, at_ref, h0_ref, d_ref, y_ref, hf_ref,
                xs_ref, ys_ref, g_ref, hp_ref, bexp_ref, cexp_ref, yi_ref, *, n_prompt_chunks):
    t = SSM_CHUNK
    nq = SSM_POS_PER_DOT
    qw = nq * LANES
    nch = x_ref.shape[0] // t
    ns = nch - n_prompt_chunks
    sl = STATE_LANES
    ii = lambda shape, dim: lax.broadcasted_iota(jnp.int32, shape, dim)

    half = ii((qw, LANES), 1) // SSM_STATE
    row_g = (ii((qw, LANES), 0) % LANES) // SSM_GROUP
    for b in range(t // nq):
        rows = slice(b * qw, (b + 1) * qw)
        for lt in range(2 * sl // LANES):
            r, g0 = divmod(lt * LANES // SSM_STATE, GROUPS_PER_TILE)
            src = q_ref[0, rows, r * LANES:(r + 1) * LANES]
            bexp_ref[rows, lt * LANES:(lt + 1) * LANES] = jnp.where(row_g == g0 + half, src, jnp.zeros_like(src))
    lane_g = (ii((SSM_STATE, t * LANES), 1) % LANES) // SSM_GROUP
    for rb in range(2 * sl // SSM_STATE):
        r, g0 = divmod(rb, GROUPS_PER_TILE)
        src = cat_ref[0, r * SSM_STATE:(r + 1) * SSM_STATE, :]
        cexp_ref[rb * SSM_STATE:(rb + 1) * SSM_STATE, :] = jnp.where(lane_g == g0, src, jnp.zeros_like(src))
    kr = lax.dot_general(q_ref[0], ca_ref[...], (((1,), (1,)), ((), ())), preferred_element_type=F32)
    same = (ii((LANES, LANES), 0) // SSM_GROUP) == (ii((LANES, LANES), 1) // SSM_GROUP)
    krev = [jnp.where(same, kr[j * LANES:(j + 1) * LANES, :], 0.0).astype(BF16) for j in range(t)]

    xs_ref[...] = x_ref[...].astype(F32)
    us = [xs_ref[pl.ds(j, nch, stride=t), :] for j in range(t)]
    un = jnp.concatenate([u.astype(BF16) for u in us], axis=1)
    g_ref[...] = jnp.dot(un, bexp_ref[...], preferred_element_type=F32)

    zblk = jnp.zeros((LANES, LANES), BF16)
    for b in range(t // nq):
        j_hi = (b + 1) * nq
        cols = []
        for j in range(b * nq, j_hi):
            cols.append(jnp.concatenate(krev[t - 1 - j:] + [zblk] * (j_hi - 1 - j), axis=0))
        kq = jnp.concatenate(cols, axis=1)
        yi_ref[:, b * qw:(b + 1) * qw] = jnp.dot(un[:, :j_hi * LANES], kq, preferred_element_type=F32)

    at = at_ref[0]
    a_re, a_im = at[:, :sl], at[:, sl:]
    rid = ii((SUBLANES, sl), 0)
    pw = [(a_re, a_im)]
    for _ in range(SUBLANES - 1):
        pw.append(_cmul(pw[-1][0], pw[-1][1], a_re, a_im))
    car_re = jnp.concatenate([p[0] for p in pw], axis=0)
    car_im = jnp.concatenate([p[1] for p in pw], axis=0)
    levels = []
    s = 1
    while s < SUBLANES:
        levels.append((s, jnp.where(rid >= s, pw[s - 1][0], 0.0), jnp.where(rid >= s, pw[s - 1][1], 0.0)))
        s *= 2

    def block(b, carry):
        c_re, c_im = carry
        r0 = b * SUBLANES
        g = g_ref[pl.ds(r0, SUBLANES), :]
        h_re, h_im = g[:, :sl], g[:, sl:]
        for s, l_re, l_im in levels:
            s_re, s_im = pltpu.roll(h_re, s, 0), pltpu.roll(h_im, s, 0)
            d_re, d_im = _cmul(l_re, l_im, s_re, s_im)
            h_re, h_im = h_re + d_re, h_im + d_im
        d_re, d_im = _cmul(car_re, car_im, c_re, c_im)
        h_re, h_im = h_re + d_re, h_im + d_im
        p_re = jnp.where(rid == 0, c_re, pltpu.roll(h_re, 1, 0))
        p_im = jnp.where(rid == 0, c_im, pltpu.roll(h_im, 1, 0))
        hp_ref[pl.ds(r0, SUBLANES), :] = jnp.concatenate([p_re, p_im], axis=1)
        return h_re[SUBLANES - 1:, :], h_im[SUBLANES - 1:, :]

    e_re = e_im = jnp.zeros((1, sl), F32)
    for b in range(n_prompt_chunks // SUBLANES):
        e_re, e_im = block(b, (e_re, e_im))

    h0 = h0_ref[0]
    hp_ref[n_prompt_chunks:, :] = h0
    gs = g_ref[n_prompt_chunks:, :]
    s_re, s_im = _cmul(a_re, a_im, h0[:, :sl], h0[:, sl:])
    hf_ref[0, 0:ns, :] = jnp.concatenate([s_re + gs[:, :sl], s_im + gs[:, sl:]], axis=1)
    hf_ref[0, ns:, :] = jnp.broadcast_to(jnp.concatenate([e_re, e_im], axis=1), (hf_ref.shape[1] - ns, 2 * sl))

    hp = hp_ref[...].astype(BF16)
    dsk = d_ref[...]
    for b in range(t // nq):
        y = yi_ref[:, b * qw:(b + 1) * qw]
        y = y + jnp.dot(hp, cexp_ref[:, b * qw:(b + 1) * qw], preferred_element_type=F32)
        for jj in range(nq):
            j = b * nq + jj
            ys_ref[pl.ds(j, nch, stride=t), :] = _gelu_tanh(y[:, jj * LANES:(jj + 1) * LANES] + dsk * us[j])
    y_ref[...] = ys_ref[...].astype(y_ref.dtype)


def _ssm(xn, tables, h0, d_skip, n_prompt_chunks):
    q, cat, c_sel, a_t = tables
    m, d = xn.shape
    tiles = d // LANES
    t = SSM_CHUNK
    nch = m // t
    ns = nch - n_prompt_chunks
    assert n_prompt_chunks % SUBLANES == 0 and ns % SUBLANES == 0 and t % SSM_POS_PER_DOT == 0
    hf_rows = 2 * ns
    tl = t * LANES
    s2 = 2 * STATE_LANES
    return pl.pallas_call(
        functools.partial(_ssm_kernel, n_prompt_chunks=n_prompt_chunks),
        grid=(tiles,),
        in_specs=[pl.BlockSpec((m, LANES), lambda i: (0, i)),
                  pl.BlockSpec((1, tl, 2 * LANES), lambda i: (i, 0, 0)),
                  pl.BlockSpec((1, LANES, tl), lambda i: (i, 0, 0)),
                  pl.BlockSpec((LANES, 2 * LANES), lambda i: (i, 0)),
                  pl.BlockSpec((1, 1, s2), lambda i: (i, 0, 0)),
                  pl.BlockSpec((1, ns, s2), lambda i: (i, 0, 0)),
                  pl.BlockSpec((1, LANES), lambda i: (0, i))],
        out_specs=[pl.BlockSpec((m, LANES), lambda i: (0, i)),
                   pl.BlockSpec((1, hf_rows, s2), lambda i: (i, 0, 0))],
        out_shape=[jax.ShapeDtypeStruct((m, d), BF16),
                   jax.ShapeDtypeStruct((tiles, hf_rows, s2), F32)],
        scratch_shapes=[pltpu.VMEM((m, LANES), F32),
                        pltpu.VMEM((m, LANES), F32),
                        pltpu.VMEM((nch, s2), F32),
                        pltpu.VMEM((nch, s2), F32),
                        pltpu.VMEM((tl, s2), BF16),
                        pltpu.VMEM((s2, tl), BF16),
                        pltpu.VMEM((nch, tl), F32)],
        compiler_params=_cparams(("parallel",)),
        name="ssm",
    )(xn, q, cat, c_sel, a_t, h0, d_skip.reshape(1, d))


def _state_to_tiles(h_re, h_im):
    b, g, p = h_re.shape
    tiles = g // GROUPS_PER_TILE
    re = h_re.reshape(b, tiles, STATE_LANES).transpose(1, 0, 2)
    im = h_im.reshape(b, tiles, STATE_LANES).transpose(1, 0, 2)
    return jnp.concatenate([re, im], axis=-1)


def _tiles_to_state(h):
    tiles, b, _ = h.shape
    re = h[..., :STATE_LANES].transpose(1, 0, 2).reshape(b, tiles * GROUPS_PER_TILE, SSM_STATE)
    im = h[..., STATE_LANES:].transpose(1, 0, 2).reshape(b, tiles * GROUPS_PER_TILE, SSM_STATE)
    return re, im


UP_ROWS, UP_COLS = 2080, 512
DOWN_ROWS, DOWN_COLS, DOWN_DEPTH = 2080, 1024, 2048
GLU_ROWS, GLU_COLS = 1664, 256
ADD_NORM_TILE, NORM_TILE = 640, 832
POOL_TILE = 256


def kernel(x_prompt, x_sample, cache_pool, state_ssm_re, state_ssm_im, norm_mix, w_pool, pool_scale, ssm_a_re, ssm_a_im, ssm_log_dt, ssm_b_re, ssm_b_im, ssm_c_re, ssm_c_im, ssm_d, w_glu_a, w_glu_b, norm_ffn, w_up, w_down, norm_final):
    bp, lp, d = x_prompt.shape
    bs, ls, _ = x_sample.shape
    assert bp == 1 and ls == SSM_CHUNK and ls == HIST_PAD and lp % SSM_CHUNK == 0
    ms = bs * ls

    hist_s = jnp.pad(cache_pool[0], ((0, 0), (1, 0), (0, 0))).reshape(ms, d)
    x1, xn, hist_p, xn_s = _pool_layer(x_prompt[0], x_sample.reshape(ms, d), hist_s, norm_mix[0],
                                       w_pool[0], pool_scale[0], norm_ffn[0], POOL_TILE, ls)
    pool_rows_p = hist_p[1:].reshape(1, 1, POOL_HIST, d)
    pool_rows_s = xn_s.reshape(bs, ls, d)[:, 1:].reshape(1, bs, POOL_HIST, d)

    h = _mlp_up(xn, w_up, 0, UP_ROWS, UP_COLS)
    dx1 = _mlp_down(h, w_down, 0, DOWN_ROWS, DOWN_COLS, DOWN_DEPTH)

    xn = _add_norm(x1, dx1, norm_mix[1], BF16, ADD_NORM_TILE)
    tables = _ssm_prep(ssm_a_re[0], ssm_a_im[0], ssm_log_dt[0], ssm_b_re[0], ssm_b_im[0],
                       ssm_c_re[0], ssm_c_im[0])
    h0 = _state_to_tiles(state_ssm_re[0], state_ssm_im[0])
    y, hf = _ssm(xn, tables, h0, ssm_d[0], lp // SSM_CHUNK)
    x3 = _glu(y, w_glu_a, w_glu_b, 0, x1, dx1, GLU_ROWS, GLU_COLS)
    re_s, im_s = _tiles_to_state(hf[:, :bs])
    re_p, im_p = _tiles_to_state(hf[:, bs:bs + 1])

    xn = _rmsnorm(x3, norm_ffn[1], BF16, NORM_TILE)
    h = _mlp_up(xn, w_up, 1, UP_ROWS, UP_COLS)
    dx3 = _mlp_down(h, w_down, 1, DOWN_ROWS, DOWN_COLS, DOWN_DEPTH)
    y_p = _add_norm(x3, dx3, norm_final, F32, 512, 0, lp).reshape(1, lp, d)
    y_s = _add_norm(x3, dx3, norm_final, F32, ms, lp, ms).reshape(bs, ls, d)
    return (y_p, y_s, pool_rows_p, pool_rows_s, re_p[None], im_p[None], re_s[None], im_s[None])
```

```python
import functools
import math

import jax
import jax.numpy as jnp
from jax import lax
from jax.experimental import pallas as pl
from jax.experimental.pallas import tpu as pltpu

F32 = jnp.float32
BF16 = jnp.bfloat16

EPS = 1e-6
PAST_LEN = 1024
POOL_WINDOWS = (2, 4, 8, 16)
POOL_HIST = max(POOL_WINDOWS) - 1
HIST_PAD = POOL_HIST + 1
SSM_GROUP = 16
SSM_STATE = 64
SSM_CHUNK = 16
LANES = 128
SUBLANES = 8
GROUPS_PER_TILE = LANES // SSM_GROUP
STATE_LANES = GROUPS_PER_TILE * SSM_STATE
VMEM_LIMIT = 56 * 1024 * 1024
VMEM_LIMIT_UP = 60 * 1024 * 1024
VMEM_LIMIT_BIG = 62 * 1024 * 1024


def _cparams(sem, vmem_limit=VMEM_LIMIT):
    return pltpu.CompilerParams(dimension_semantics=sem, vmem_limit_bytes=vmem_limit)


def _rms(x, g):
    return x * lax.rsqrt(jnp.mean(x * x, axis=-1, keepdims=True) + EPS) * g


def _rmsnorm_kernel(x_ref, g_ref, o_ref):
    o_ref[...] = _rms(x_ref[...], g_ref[...]).astype(o_ref.dtype)


def _rmsnorm(x, g, out_dtype, tm):
    m, d = x.shape
    return pl.pallas_call(
        _rmsnorm_kernel,
        grid=(m // tm,),
        in_specs=[pl.BlockSpec((tm, d), lambda i: (i, 0)),
                  pl.BlockSpec((1, d), lambda i: (0, 0))],
        out_specs=pl.BlockSpec((tm, d), lambda i: (i, 0)),
        out_shape=jax.ShapeDtypeStruct((m, d), out_dtype),
        compiler_params=_cparams(("parallel",)),
        name="rmsnorm",
    )(x, g.reshape(1, d))


def _add_norm_kernel(x_ref, dx_ref, g_ref, o_ref):
    o_ref[...] = _rms(x_ref[...] + dx_ref[...].astype(F32), g_ref[...]).astype(o_ref.dtype)


def _add_norm(x, dx, g, out_dtype, tm, row0=0, rows=None):
    m, d = x.shape
    rows = m - row0 if rows is None else rows
    assert rows % tm == 0 and row0 % tm == 0
    off = row0 // tm
    src = pl.BlockSpec((tm, d), lambda i: (i + off, 0))
    return pl.pallas_call(
        _add_norm_kernel,
        grid=(rows // tm,),
        in_specs=[src, src, pl.BlockSpec((1, d), lambda i: (0, 0))],
        out_specs=pl.BlockSpec((tm, d), lambda i: (i, 0)),
        out_shape=jax.ShapeDtypeStruct((rows, d), out_dtype),
        compiler_params=_cparams(("parallel",)),
        name="add_norm",
    )(x, dx, g.reshape(1, d))


def _pool_finish(x, a, s_scaled, gi, pg, w_ref, sc_ref, x1_ref, rows):
    sl = slice(gi * pg, (gi + 1) * pg)
    pooled = s_scaled - a
    mixed = jnp.dot(pooled.astype(BF16), w_ref[gi], preferred_element_type=F32)
    x1 = x[:, sl] + mixed * sc_ref[:, sl]
    x1_ref[0:rows, sl] = x1
    return jnp.sum(x1 * x1, axis=-1, keepdims=True)


def _pool_kernel(xp_ref, xs_ref, hs_ref, gm_ref, wf_ref, sc_ref, gf_ref,
                 x1_ref, xnf_ref, hist_ref, xns_ref, ext_ref, w_ref, *, tm, n_prompt_tiles, seq):
    i = pl.program_id(0)
    d = xp_ref.shape[1]
    pg = d // len(POOL_WINDOWS)

    @pl.when(i == 0)
    def _():
        ext_ref[0:HIST_PAD, :] = jnp.zeros((HIST_PAD, d), F32)
        for gi in range(len(POOL_WINDOWS)):
            w_ref[gi] = wf_ref[gi].astype(BF16)

    @pl.when(i < n_prompt_tiles)
    def _():
        x = xp_ref[...]
        ext_ref[HIST_PAD:, :] = _rms(x, gm_ref[...])
        pos = i * tm + lax.broadcasted_iota(jnp.int32, (tm, 1), 0)
        ss = jnp.zeros((tm, 1), F32)
        for gi, w in enumerate(POOL_WINDOWS):
            a = ext_ref[:, gi * pg:(gi + 1) * pg]
            s = a
            span = 1
            while span < w:
                s = s + pltpu.roll(s, span, 0)
                span *= 2
            inv_cnt = 1.0 / jnp.minimum(w, pos + 1).astype(F32)
            ss = ss + _pool_finish(x, a[HIST_PAD:, :], s[HIST_PAD:, :] * inv_cnt, gi, pg, w_ref, sc_ref, x1_ref, tm)
        inv = lax.rsqrt(ss / d + EPS)
        xnf_ref[...] = (x1_ref[...] * inv * gf_ref[...]).astype(xnf_ref.dtype)
        hist_ref[...] = ext_ref[tm:, :]
        ext_ref[0:HIST_PAD, :] = ext_ref[tm:, :]

    @pl.when(i == n_prompt_tiles)
    def _():
        ms = xs_ref.shape[0]
        x = xs_ref[...]
        xns_ref[...] = _rms(x, gm_ref[...])
        t_in_seq = lax.broadcasted_iota(jnp.int32, (ms, 1), 0) % seq
        ss = jnp.zeros((ms, 1), F32)
        for gi, w in enumerate(POOL_WINDOWS):
            sl = slice(gi * pg, (gi + 1) * pg)
            a = xns_ref[:, sl]
            h = hs_ref[:, sl]
            s = a
            for lag in range(1, w):
                cur = pltpu.roll(a, lag, 0)
                old = pltpu.roll(h, (ms - HIST_PAD + lag) % ms, 0)
                s = s + jnp.where(t_in_seq >= lag, cur, old)
            ss = ss + _pool_finish(x, a, s / float(w), gi, pg, w_ref, sc_ref, x1_ref, ms)
        inv = lax.rsqrt(ss / d + EPS)
        xnf_ref[0:ms, :] = (x1_ref[0:ms, :] * inv * gf_ref[...]).astype(xnf_ref.dtype)


def _pool_layer(x_prompt, x_sample, hist_sample, g_mix, w_pool, scale, g_ffn, tm, seq):
    l, d = x_prompt.shape
    ms = x_sample.shape[0]
    assert l % tm == 0 and ms <= tm and PAST_LEN >= POOL_HIST
    nt = l // tm
    ng, pg, _ = w_pool.shape
    vec = pl.BlockSpec((1, d), lambda i: (0, 0))
    whole = pl.BlockSpec((ms, d), lambda i: (0, 0))
    tile = pl.BlockSpec((tm, d), lambda i: (i, 0))
    return pl.pallas_call(
        functools.partial(_pool_kernel, tm=tm, n_prompt_tiles=nt, seq=seq),
        grid=(nt + 1,),
        in_specs=[pl.BlockSpec((tm, d), lambda i: (jnp.minimum(i, nt - 1), 0)), whole, whole, vec,
                  pl.BlockSpec((ng, pg, pg), lambda i: (0, 0, 0), pipeline_mode=pl.Buffered(1)), vec, vec],
        out_specs=[tile, tile, pl.BlockSpec((HIST_PAD, d), lambda i: (0, 0)), whole],
        out_shape=[jax.ShapeDtypeStruct((l + ms, d), F32),
                   jax.ShapeDtypeStruct((l + ms, d), BF16),
                   jax.ShapeDtypeStruct((HIST_PAD, d), F32),
                   jax.ShapeDtypeStruct((ms, d), F32)],
        scratch_shapes=[pltpu.VMEM((HIST_PAD + tm, d), F32), pltpu.VMEM((ng, pg, pg), BF16)],
        compiler_params=_cparams(("arbitrary",)),
        name="pool_layer",
    )(x_prompt, x_sample, hist_sample, g_mix.reshape(1, d), w_pool, scale.reshape(1, d), g_ffn.reshape(1, d))


def _up_kernel(x_ref, w_ref, o_ref):
    acc = jnp.dot(x_ref[...], w_ref[...].astype(BF16), preferred_element_type=F32)
    h = jnp.maximum(acc, 0.0)
    o_ref[...] = (h * h).astype(o_ref.dtype)


def _mlp_up(xn, w, layer, tm, tn):
    m, k = xn.shape
    n = w.shape[2]
    return pl.pallas_call(
        _up_kernel,
        grid=(m // tm, n // tn),
        in_specs=[pl.BlockSpec((tm, k), lambda i, j: (i, 0)),
                  pl.BlockSpec((None, k, tn), lambda i, j: (layer, 0, j))],
        out_specs=pl.BlockSpec((tm, tn), lambda i, j: (i, j)),
        out_shape=jax.ShapeDtypeStruct((m, n), BF16),
        compiler_params=_cparams(("parallel", "arbitrary"), VMEM_LIMIT_UP),
        name="mlp_up",
    )(xn, w)


def _down_kernel(h_ref, w_ref, o_ref, acc_ref):
    kk = pl.program_id(2)

    @pl.when(kk == 0)
    def _():
        acc_ref[...] = jnp.zeros_like(acc_ref)

    acc_ref[...] += jnp.dot(h_ref[...], w_ref[...].astype(BF16), preferred_element_type=F32)

    @pl.when(kk == pl.num_programs(2) - 1)
    def _():
        o_ref[...] = acc_ref[...].astype(o_ref.dtype)


def _mlp_down(h, w, layer, tm, tn, tk):
    m, k = h.shape
    n = w.shape[2]
    return pl.pallas_call(
        _down_kernel,
        grid=(m // tm, n // tn, k // tk),
        in_specs=[pl.BlockSpec((tm, tk), lambda i, j, kk: (i, kk)),
                  pl.BlockSpec((None, tk, tn), lambda i, j, kk: (layer, kk, j))],
        out_specs=pl.BlockSpec((tm, tn), lambda i, j, kk: (i, j)),
        out_shape=jax.ShapeDtypeStruct((m, n), BF16),
        scratch_shapes=[pltpu.VMEM((tm, tn), F32)],
        compiler_params=_cparams(("parallel", "parallel", "arbitrary"), VMEM_LIMIT_BIG),
        name="mlp_down",
    )(h, w)


def _glu_kernel(y_ref, wa_ref, wb_ref, r_ref, dr_ref, o_ref):
    y = y_ref[...]
    a = jnp.dot(y, wa_ref[...].astype(BF16), preferred_element_type=F32)
    b = jnp.dot(y, wb_ref[...].astype(BF16), preferred_element_type=F32)
    o_ref[...] = (r_ref[...] + dr_ref[...].astype(F32)) + a * jax.nn.sigmoid(b)


def _glu(y, wa, wb, layer, resid, dresid, tm, tn):
    m, k = y.shape
    n = wa.shape[2]
    wspec = pl.BlockSpec((None, k, tn), lambda i, j: (layer, 0, j))
    tile = pl.BlockSpec((tm, tn), lambda i, j: (i, j))
    return pl.pallas_call(
        _glu_kernel,
        grid=(m // tm, n // tn),
        in_specs=[pl.BlockSpec((tm, k), lambda i, j: (i, 0)),
                  wspec, wspec, tile, tile],
        out_specs=tile,
        out_shape=jax.ShapeDtypeStruct((m, n), F32),
        compiler_params=_cparams(("parallel", "arbitrary"), VMEM_LIMIT_BIG),
        name="glu",
    )(y, wa, wb, resid, dresid)


def _cmul(a_re, a_im, b_re, b_im):
    return a_re * b_re - a_im * b_im, a_re * b_im + a_im * b_re


def _ssm_prep_kernel(are_ref, aim_ref, ldt_ref, brr_ref, bii_ref, ca_ref, cb_ref, q_ref, cat_ref, pt_ref):
    t = SSM_CHUNK
    rows = brr_ref.shape[0]

    def per_channel(v):
        return jnp.concatenate([jnp.broadcast_to(v[g:g + 1, :], (SSM_GROUP, v.shape[1])) for g in range(v.shape[0])],
                               axis=0)

    l_re, l_im = are_ref[...], aim_ref[...]
    dt = jnp.exp(ldt_ref[...])
    mag = jnp.exp(l_re * dt)
    a_re, a_im = mag * jnp.cos(l_im * dt), mag * jnp.sin(l_im * dt)
    den = l_re * l_re + l_im * l_im
    n_re, n_im = a_re - 1.0, a_im
    k_re = (n_re * l_re + n_im * l_im) / den
    k_im = (n_im * l_re - n_re * l_im) / den
    bb_re, bb_im = _cmul(per_channel(k_re), per_channel(k_im), brr_ref[...], bii_ref[...])
    c_a, c_b = ca_ref[...], cb_ref[...]
    p_re, p_im = jnp.ones_like(a_re), jnp.zeros_like(a_re)
    for n in range(t):
        j = t - 1 - n
        q_re, q_im = _cmul(per_channel(p_re), per_channel(p_im), bb_re, bb_im)
        q_ref[0, j * rows:(j + 1) * rows, 0:LANES] = q_re.astype(q_ref.dtype)
        q_ref[0, j * rows:(j + 1) * rows, LANES:] = q_im.astype(q_ref.dtype)
        p_re, p_im = _cmul(p_re, p_im, a_re, a_im)
        cat = per_channel(p_re) * c_a + per_channel(p_im) * c_b
        cat_ref[0, :, n * rows:(n + 1) * rows] = cat.T.astype(cat_ref.dtype)
    half = lax.broadcasted_iota(jnp.int32, p_re.shape, 1) < (p_re.shape[1] // 2)
    pt_ref[...] = jnp.where(half, p_re, p_im)


def _ssm_prep(a_re, a_im, log_dt, b_re, b_im, c_re, c_im):
    g, p = a_re.shape
    c = b_re.shape[-1]
    t = SSM_CHUNK
    tiles = g // GROUPS_PER_TILE
    rows = g * c
    assert 2 * p == LANES and c == SSM_GROUP
    dup = lambda v: jnp.concatenate([v, v], axis=-1)
    bt_re = b_re.transpose(0, 2, 1).reshape(rows, p)
    bt_im = b_im.transpose(0, 2, 1).reshape(rows, p)
    cr, ci = c_re.reshape(rows, p), c_im.reshape(rows, p)
    args = (dup(a_re), dup(a_im), dup(jnp.broadcast_to(log_dt[:, None], (g, p))),
            dup(bt_re), dup(bt_im),
            jnp.concatenate([cr, -ci], axis=-1), jnp.concatenate([-ci, -cr], axis=-1))
    grp = pl.BlockSpec((GROUPS_PER_TILE, LANES), lambda i: (i, 0))
    blk = pl.BlockSpec((LANES, LANES), lambda i: (i, 0))
    q, cat, pt = pl.pallas_call(
        _ssm_prep_kernel,
        grid=(tiles,),
        in_specs=[grp] * 3 + [blk] * 4,
        out_specs=[pl.BlockSpec((1, t * LANES, 2 * LANES), lambda i: (i, 0, 0)),
                   pl.BlockSpec((1, LANES, t * LANES), lambda i: (i, 0, 0)),
                   grp],
        out_shape=[jax.ShapeDtypeStruct((tiles, t * LANES, 2 * LANES), BF16),
                   jax.ShapeDtypeStruct((tiles, LANES, t * LANES), BF16),
                   jax.ShapeDtypeStruct((g, LANES), F32)],
        compiler_params=_cparams(("parallel",)),
        name="ssm_prep",
    )(*args)
    a_t = pt.reshape(tiles, GROUPS_PER_TILE, 2, p).transpose(0, 2, 1, 3).reshape(tiles, 1, 2 * STATE_LANES)
    zero = jnp.zeros_like(cr)
    c_sel = jnp.concatenate([cr, zero, -ci, zero], axis=-1).astype(BF16)
    return q, cat, c_sel, a_t


def _gelu_tanh(x):
    return 0.5 * x * (1.0 + jnp.tanh(math.sqrt(2.0 / math.pi) * (x + 0.044715 * x * x * x)))


SSM_POS_PER_DOT = 4


def _ssm_kernel(x_ref, q_ref, cat_ref, ca_ref, at_ref, h0_ref, d_ref, y_ref, hf_ref,
                xs_ref, ys_ref, g_ref, hp_ref, bexp_ref, cexp_ref, yi_ref, un_ref, *, n_prompt_chunks):
    t = SSM_CHUNK
    nq = SSM_POS_PER_DOT
    qw = nq * LANES
    nch = x_ref.shape[0] // t
    ns = nch - n_prompt_chunks
    sl = STATE_LANES
    ii = lambda shape, dim: lax.broadcasted_iota(jnp.int32, shape, dim)

    half = ii((qw, LANES), 1) // SSM_STATE
    row_g = (ii((qw, LANES), 0) % LANES) // SSM_GROUP
    for b in range(t // nq):
        rows = slice(b * qw, (b + 1) * qw)
        for lt in range(2 * sl // LANES):
            r, g0 = divmod(lt * LANES // SSM_STATE, GROUPS_PER_TILE)
            src = q_ref[0, rows, r * LANES:(r + 1) * LANES]
            bexp_ref[rows, lt * LANES:(lt + 1) * LANES] = jnp.where(row_g == g0 + half, src, jnp.zeros_like(src))
    lane_g = (ii((SSM_STATE, t * LANES), 1) % LANES) // SSM_GROUP
    for rb in range(2 * sl // SSM_STATE):
        r, g0 = divmod(rb, GROUPS_PER_TILE)
        src = cat_ref[0, r * SSM_STATE:(r + 1) * SSM_STATE, :]
        cexp_ref[rb * SSM_STATE:(rb + 1) * SSM_STATE, :] = jnp.where(lane_g == g0, src, jnp.zeros_like(src))
    kr = lax.dot_general(q_ref[0], ca_ref[...], (((1,), (1,)), ((), ())), preferred_element_type=F32)
    same = (ii((LANES, LANES), 0) // SSM_GROUP) == (ii((LANES, LANES), 1) // SSM_GROUP)
    krev = [jnp.where(same, kr[j * LANES:(j + 1) * LANES, :], 0.0).astype(BF16) for j in range(t)]

    xs_ref[...] = x_ref[...].astype(F32)
    un_ref[...] = jnp.concatenate([xs_ref[pl.ds(j, nch, stride=t), :].astype(BF16) for j in range(t)], axis=1)
    g_ref[...] = jnp.dot(un_ref[...], bexp_ref[...], preferred_element_type=F32)

    zblk = jnp.zeros((LANES, LANES), BF16)
    for b in range(t // nq):
        j_hi = (b + 1) * nq
        cols = []
        for j in range(b * nq, j_hi):
            cols.append(jnp.concatenate(krev[t - 1 - j:] + [zblk] * (j_hi - 1 - j), axis=0))
        kq = jnp.concatenate(cols, axis=1)
        yi_ref[:, b * qw:(b + 1) * qw] = jnp.dot(un_ref[:, :j_hi * LANES], kq, preferred_element_type=F32)

    at = at_ref[0]
    a_re, a_im = at[:, :sl], at[:, sl:]
    rid = ii((SUBLANES, sl), 0)
    pw = [(a_re, a_im)]
    for _ in range(SUBLANES - 1):
        pw.append(_cmul(pw[-1][0], pw[-1][1], a_re, a_im))
    car_re = jnp.concatenate([p[0] for p in pw], axis=0)
    car_im = jnp.concatenate([p[1] for p in pw], axis=0)
    levels = []
    s = 1
    while s < SUBLANES:
        levels.append((s, jnp.where(rid >= s, pw[s - 1][0], 0.0), jnp.where(rid >= s, pw[s - 1][1], 0.0)))
        s *= 2

    def block(b, carry):
        c_re, c_im = carry
        r0 = b * SUBLANES
        g = g_ref[pl.ds(r0, SUBLANES), :]
        h_re, h_im = g[:, :sl], g[:, sl:]
        for s, l_re, l_im in levels:
            s_re, s_im = pltpu.roll(h_re, s, 0), pltpu.roll(h_im, s, 0)
            d_re, d_im = _cmul(l_re, l_im, s_re, s_im)
            h_re, h_im = h_re + d_re, h_im + d_im
        d_re, d_im = _cmul(car_re, car_im, c_re, c_im)
        h_re, h_im = h_re + d_re, h_im + d_im
        p_re = jnp.where(rid == 0, c_re, pltpu.roll(h_re, 1, 0))
        p_im = jnp.where(rid == 0, c_im, pltpu.roll(h_im, 1, 0))
        hp_ref[pl.ds(r0, SUBLANES), :] = jnp.concatenate([p_re, p_im], axis=1)
        return h_re[SUBLANES - 1:, :], h_im[SUBLANES - 1:, :]

    e_re = e_im = jnp.zeros((1, sl), F32)
    for b in range(n_prompt_chunks // SUBLANES):
        e_re, e_im = block(b, (e_re, e_im))

    h0 = h0_ref[0]
    hp_ref[n_prompt_chunks:, :] = h0
    gs = g_ref[n_prompt_chunks:, :]
    s_re, s_im = _cmul(a_re, a_im, h0[:, :sl], h0[:, sl:])
    hf_ref[0, 0:ns, :] = jnp.concatenate([s_re + gs[:, :sl], s_im + gs[:, sl:]], axis=1)
    hf_ref[0, ns:, :] = jnp.broadcast_to(jnp.concatenate([e_re, e_im], axis=1), (hf_ref.shape[1] - ns, 2 * sl))

    hp = hp_ref[...].astype(BF16)
    dsk = d_ref[...]
    for b in range(t // nq):
        y = yi_ref[:, b * qw:(b + 1) * qw]
        y = y + jnp.dot(hp, cexp_ref[:, b * qw:(b + 1) * qw], preferred_element_type=F32)
        for jj in range(nq):
            j = b * nq + jj
            u = xs_ref[pl.ds(j, nch, stride=t), :]
            ys_ref[pl.ds(j, nch, stride=t), :] = _gelu_tanh(y[:, jj * LANES:(jj + 1) * LANES] + dsk * u)
    y_ref[...] = ys_ref[...].astype(y_ref.dtype)


def _ssm(xn, tables, h0, d_skip, n_prompt_chunks):
    q, cat, c_sel, a_t = tables
    m, d = xn.shape
    tiles = d // LANES
    t = SSM_CHUNK
    nch = m // t
    ns = nch - n_prompt_chunks
    assert n_prompt_chunks % SUBLANES == 0 and ns % SUBLANES == 0 and t % SSM_POS_PER_DOT == 0
    hf_rows = 2 * ns
    tl = t * LANES
    s2 = 2 * STATE_LANES
    return pl.pallas_call(
        functools.partial(_ssm_kernel, n_prompt_chunks=n_prompt_chunks),
        grid=(tiles,),
        in_specs=[pl.BlockSpec((m, LANES), lambda i: (0, i)),
                  pl.BlockSpec((1, tl, 2 * LANES), lambda i: (i, 0, 0)),
                  pl.BlockSpec((1, LANES, tl), lambda i: (i, 0, 0)),
                  pl.BlockSpec((LANES, 2 * LANES), lambda i: (i, 0)),
                  pl.BlockSpec((1, 1, s2), lambda i: (i, 0, 0)),
                  pl.BlockSpec((1, ns, s2), lambda i: (i, 0, 0)),
                  pl.BlockSpec((1, LANES), lambda i: (0, i))],
        out_specs=[pl.BlockSpec((m, LANES), lambda i: (0, i)),
                   pl.BlockSpec((1, hf_rows, s2), lambda i: (i, 0, 0))],
        out_shape=[jax.ShapeDtypeStruct((m, d), BF16),
                   jax.ShapeDtypeStruct((tiles, hf_rows, s2), F32)],
        scratch_shapes=[pltpu.VMEM((m, LANES), F32),
                        pltpu.VMEM((m, LANES), F32),
                        pltpu.VMEM((nch, s2), F32),
                        pltpu.VMEM((nch, s2), F32),
                        pltpu.VMEM((tl, s2), BF16),
                        pltpu.VMEM((s2, tl), BF16),
                        pltpu.VMEM((nch, tl), F32),
                        pltpu.VMEM((nch, tl), BF16)],
        compiler_params=_cparams(("parallel",)),
        name="ssm",
    )(xn, q, cat, c_sel, a_t, h0, d_skip.reshape(1, d))


def _state_to_tiles(h_re, h_im):
    b, g, p = h_re.shape
    tiles = g // GROUPS_PER_TILE
    re = h_re.reshape(b, tiles, STATE_LANES).transpose(1, 0, 2)
    im = h_im.reshape(b, tiles, STATE_LANES).transpose(1, 0, 2)
    return jnp.concatenate([re, im], axis=-1)


def _tiles_to_state(h):
    tiles, b, _ = h.shape
    re = h[..., :STATE_LANES].transpose(1, 0, 2).reshape(b, tiles * GROUPS_PER_TILE, SSM_STATE)
    im = h[..., STATE_LANES:].transpose(1, 0, 2).reshape(b, tiles * GROUPS_PER_TILE, SSM_STATE)
    return re, im


UP_ROWS, UP_COLS = 2080, 512
DOWN_ROWS, DOWN_COLS, DOWN_DEPTH = 2080, 1024, 2048
GLU_ROWS, GLU_COLS = 1664, 256
ADD_NORM_TILE, NORM_TILE = 640, 832
POOL_TILE = 256


def kernel(x_prompt, x_sample, cache_pool, state_ssm_re, state_ssm_im, norm_mix, w_pool, pool_scale, ssm_a_re, ssm_a_im, ssm_log_dt, ssm_b_re, ssm_b_im, ssm_c_re, ssm_c_im, ssm_d, w_glu_a, w_glu_b, norm_ffn, w_up, w_down, norm_final):
    bp, lp, d = x_prompt.shape
    bs, ls, _ = x_sample.shape
    assert bp == 1 and ls == SSM_CHUNK and ls == HIST_PAD and lp % SSM_CHUNK == 0
    ms = bs * ls

    hist_s = jnp.pad(cache_pool[0], ((0, 0), (1, 0), (0, 0))).reshape(ms, d)
    x1, xn, hist_p, xn_s = _pool_layer(x_prompt[0], x_sample.reshape(ms, d), hist_s, norm_mix[0],
                                       w_pool[0], pool_scale[0], norm_ffn[0], POOL_TILE, ls)
    pool_rows_p = hist_p[1:].reshape(1, 1, POOL_HIST, d)
    pool_rows_s = xn_s.reshape(bs, ls, d)[:, 1:].reshape(1, bs, POOL_HIST, d)

    h = _mlp_up(xn, w_up, 0, UP_ROWS, UP_COLS)
    dx1 = _mlp_down(h, w_down, 0, DOWN_ROWS, DOWN_COLS, DOWN_DEPTH)

    xn = _add_norm(x1, dx1, norm_mix[1], BF16, ADD_NORM_TILE)
    tables = _ssm_prep(ssm_a_re[0], ssm_a_im[0], ssm_log_dt[0], ssm_b_re[0], ssm_b_im[0],
                       ssm_c_re[0], ssm_c_im[0])
    h0 = _state_to_tiles(state_ssm_re[0], state_ssm_im[0])
    y, hf = _ssm(xn, tables, h0, ssm_d[0], lp // SSM_CHUNK)
    x3 = _glu(y, w_glu_a, w_glu_b, 0, x1, dx1, GLU_ROWS, GLU_COLS)
    re_s, im_s = _tiles_to_state(hf[:, :bs])
    re_p, im_p = _tiles_to_state(hf[:, bs:bs + 1])

    xn = _rmsnorm(x3, norm_ffn[1], BF16, NORM_TILE)
    h = _mlp_up(xn, w_up, 1, UP_ROWS, UP_COLS)
    dx3 = _mlp_down(h, w_down, 1, DOWN_ROWS, DOWN_COLS, DOWN_DEPTH)
    y_p = _add_norm(x3, dx3, norm_final, F32, 512, 0, lp).reshape(1, lp, d)
    y_s = _add_norm(x3, dx3, norm_final, F32, ms, lp, ms).reshape(bs, ls, d)
    return (y_p, y_s, pool_rows_p, pool_rows_s, re_p[None], im_p[None], re_s[None], im_s[None])
```
